```python
import math
import jax
import jax.numpy as jnp
from jax import lax
import numpy as np

D_MODEL = 1024
BATCH = 8
SEQ = 2048
DEPTH = 1
DEC_BATCH = 32
DEC_SEQ = 4
PAST_LEN = 16384
PAGE_SIZE = 128

N_META = 16
N_HEADS = 8
HEAD_DIM = 64
V_DIM = 2 * HEAD_DIM
QK_W = N_HEADS * 2 * HEAD_DIM
ATTN_W = N_HEADS * V_DIM
CONV_W = D_MODEL
CONV_K = 31
N_EXPERTS = 64
N_GROUPS = 8
TOPK_GROUPS = 4
TOP_K = 8
D_EXPERT = D_MODEL // 4
ROUTED_SCALE = 2.5
Q_BLOCK = 128
LN_EPS = 1e-5
ALPHA = (2.0 * DEPTH) ** 0.25
BETA = (8.0 * DEPTH) ** -0.25
V_OFF = 2 * QK_W
C_OFF = V_OFF + ATTN_W
G_OFF = C_OFF + 2 * CONV_W
IN_W = G_OFF + 2 * D_MODEL
NEG = -1e30

kernel_name = 'hybrid_diffattn_conformer_moe_step'


def layer_norm(x, g, b):
    xf = x.astype(jnp.float32)
    mu = jnp.mean(xf, axis=-1, keepdims=True)
    var = jnp.mean(jnp.square(xf - mu), axis=-1, keepdims=True)
    y = (xf - mu) * lax.rsqrt(var + LN_EPS) * g.astype(jnp.float32) + b.astype(jnp.float32)
    return y.astype(x.dtype)


def rms_norm(x, g):
    xf = x.astype(jnp.float32)
    ms = jnp.mean(jnp.square(xf), axis=-1, keepdims=True)
    return (xf * lax.rsqrt(ms + LN_EPS) * g.astype(jnp.float32)).astype(x.dtype)


def diff_lambda(lq1, lk1, lq2, lk2, lam_init):
    f32 = jnp.float32
    return (jnp.exp(jnp.sum(lq1.astype(f32) * lk1.astype(f32)))
            - jnp.exp(jnp.sum(lq2.astype(f32) * lk2.astype(f32))) + lam_init)


def split_projection(x, w_in, b_in):
    z = x @ w_in + b_in
    lead = x.shape[:-1]
    q = z[..., :QK_W].reshape(lead + (N_HEADS, 2, HEAD_DIM))
    k = z[..., QK_W:V_OFF].reshape(lead + (N_HEADS, 2, HEAD_DIM))
    v = z[..., V_OFF:C_OFF].reshape(lead + (N_HEADS, V_DIM))
    return q, k, v, z[..., C_OFF:G_OFF], z[..., G_OFF:]


def diff_attend(q, k, v, q_pos, k_pos, lam):
    s = jnp.einsum('nqhmd,nkhmd->nhmqk', q, k, preferred_element_type=jnp.float32) * (HEAD_DIM ** -0.5)
    causal = k_pos[None, :] <= q_pos[:, None]
    p = jax.nn.softmax(jnp.where(causal, s, NEG), axis=-1)
    a = p[:, :, 0] - lam * p[:, :, 1]
    return jnp.einsum('nhqk,nkhe->nqhe', a.astype(v.dtype), v)


def conformer_conv(c, buf, conv_w, conv_b, ln_g, ln_b, w_co, b_co):
    u = c[..., :CONV_W] * jax.nn.sigmoid(c[..., CONV_W:])
    ext = jnp.concatenate([buf.astype(u.dtype), u], axis=1)
    y = lax.conv_general_dilated(ext, conv_w[:, None, :].astype(ext.dtype), window_strides=(1,),
                                 padding='VALID', dimension_numbers=('NWC', 'WIO', 'NWC'),
                                 feature_group_count=CONV_W) + conv_b
    y = jax.nn.silu(layer_norm(y, ln_g, ln_b))
    return y @ w_co + b_co, ext[:, -(CONV_K - 1):]


def moe_ffn(x, w_router, e_bias, w_gate, w_up, w_down, ws_gate, ws_up, ws_down):
    s = jax.nn.sigmoid(jnp.einsum('td,de->te', x, w_router, preferred_element_type=jnp.float32))
    sc = s + e_bias.astype(jnp.float32)
    grouped = sc.reshape(-1, N_GROUPS, N_EXPERTS // N_GROUPS)
    gscore = jnp.sum(lax.top_k(grouped, 2)[0], axis=-1)
    _, gidx = lax.top_k(gscore, TOPK_GROUPS)
    gmask = jnp.any(gidx[:, :, None] == jnp.arange(N_GROUPS)[None, None, :], axis=1)
    emask = jnp.repeat(gmask, N_EXPERTS // N_GROUPS, axis=1)
    _, eidx = lax.top_k(jnp.where(emask, sc, NEG), TOP_K)
    w = jnp.take_along_axis(s, eidx, axis=1)
    w = w / jnp.sum(w, axis=-1, keepdims=True) * ROUTED_SCALE
    gate = jnp.sum((eidx[:, :, None] == jnp.arange(N_EXPERTS)[None, None, :]) * w[:, :, None], axis=1)
    h = jax.nn.silu(jnp.einsum('td,edf->tef', x, w_gate)) * jnp.einsum('td,edf->tef', x, w_up)
    routed = jnp.einsum('tef,efd->td', h * gate[:, :, None].astype(h.dtype), w_down)
    shared = (jax.nn.silu(x @ ws_gate) * (x @ ws_up)) @ ws_down
    return routed + shared


def finish_layer(x, o, conv_out, gate_logits, p, lam_init):
    a = rms_norm(o, p['subln_g']) * (1.0 - lam_init)
    a = a.reshape(o.shape[:-2] + (ATTN_W,)) @ p['w_attn_o']
    gates = jax.nn.sigmoid(gate_logits)
    m = gates[..., :D_MODEL] * a + gates[..., D_MODEL:] * conv_out
    h = layer_norm(ALPHA * x + m @ p['w_out'], p['ln1_g'], p['ln1_b'])
    f = lax.map(lambda hs: moe_ffn(hs, p['w_router'], p['e_bias'], p['w_gate'], p['w_up'], p['w_down'],
                                   p['ws_gate'], p['ws_up'], p['ws_down']), h)
    return layer_norm(ALPHA * h + f, p['ln2_g'], p['ln2_b'])


def prompt_layer(x, p, lam_init):
    n, seq_len, _ = x.shape
    q, k, v, c, g = split_projection(x, p['w_in'], p['b_in'])
    lam = diff_lambda(p['lq1'], p['lk1'], p['lq2'], p['lk2'], lam_init)
    pos = jnp.arange(seq_len)
    o_meta = diff_attend(q[:, :N_META], k, v, pos[:N_META], pos, lam)
    n_blk = (seq_len - N_META) // Q_BLOCK

    def query_block(j):
        start = N_META + j * Q_BLOCK
        qb = lax.dynamic_slice_in_dim(q, start, Q_BLOCK, axis=1)
        return diff_attend(qb, k, v, start + jnp.arange(Q_BLOCK), pos, lam)

    o_real = lax.map(query_block, jnp.arange(n_blk))
    o_real = jnp.moveaxis(o_real, 0, 1).reshape(n, seq_len - N_META, N_HEADS, V_DIM)
    o = jnp.concatenate([o_meta, o_real], axis=1)
    buf0 = jnp.zeros((n, CONV_K - 1, CONV_W), x.dtype)
    conv_out, buf = conformer_conv(c, buf0, p['conv_w'], p['conv_b'], p['conv_ln_g'], p['conv_ln_b'],
                                   p['w_conv_o'], p['b_conv_o'])
    return finish_layer(x, o, conv_out, g, p, lam_init), k, v, buf


def sample_layer(x, k_past, v_past, conv_buf, p, lam_init):
    t = x.shape[1]
    past_len = k_past.shape[1]
    q, k, v, c, g = split_projection(x, p['w_in'], p['b_in'])
    lam = diff_lambda(p['lq1'], p['lk1'], p['lq2'], p['lk2'], lam_init)
    k_all = jnp.concatenate([k_past.astype(k.dtype), k], axis=1)
    v_all = jnp.concatenate([v_past.astype(v.dtype), v], axis=1)
    o = diff_attend(q, k_all, v_all, past_len + jnp.arange(t), jnp.arange(past_len + t), lam)
    conv_out, buf = conformer_conv(c, conv_buf, p['conv_w'], p['conv_b'], p['conv_ln_g'], p['conv_ln_b'],
                                   p['w_conv_o'], p['b_conv_o'])
    return finish_layer(x, o, conv_out, g, p, lam_init), k, v, buf


def _normal(k, shape, scale):
    return jax.random.normal(k, shape, jnp.float32) * scale


def setup_inputs(seed: int = 0) -> dict:
    key = jax.random.key(seed)
    ks = jax.random.split(key, 40)
    n_pages = PAST_LEN // PAGE_SIZE
    n_pool = (5 * DEC_BATCH * n_pages) // 4
    d, f, e = D_MODEL, D_EXPERT, N_EXPERTS
    sd = d ** -0.5
    page_table = jax.random.permutation(ks[5], n_pool)[:DEC_BATCH * n_pages].reshape(DEC_BATCH, n_pages).astype(jnp.int32)
    w_in = jnp.concatenate([
        _normal(ks[7], (DEPTH, d, QK_W), sd),
        _normal(ks[8], (DEPTH, d, QK_W), sd),
        _normal(ks[9], (DEPTH, d, ATTN_W), sd * BETA),
        _normal(ks[10], (DEPTH, d, 2 * CONV_W), sd * BETA),
        _normal(ks[11], (DEPTH, d, 2 * d), sd)], axis=-1)
    return {
        'x_prompt': _normal(ks[0], (BATCH, SEQ, d), 1.0),
        'x_sample': _normal(ks[1], (DEC_BATCH, DEC_SEQ, d), 1.0),
        'cache_k': _normal(ks[2], (DEPTH, n_pool, PAGE_SIZE, N_HEADS, 2, HEAD_DIM), 1.0),
        'cache_v': _normal(ks[3], (DEPTH, n_pool, PAGE_SIZE, N_HEADS, V_DIM), 1.0),
        'state_conv': _normal(ks[4], (DEPTH, DEC_BATCH, CONV_K - 1, CONV_W), 0.5),
        'page_table': page_table,
        'meta_tokens': _normal(ks[6], (N_META, d), 1.0),
        'w_in': w_in,
        'b_in': _normal(ks[12], (DEPTH, IN_W), 0.02),
        'lq1': _normal(ks[13], (DEPTH, HEAD_DIM), 0.1),
        'lk1': _normal(ks[14], (DEPTH, HEAD_DIM), 0.1),
        'lq2': _normal(ks[15], (DEPTH, HEAD_DIM), 0.1),
        'lk2': _normal(ks[16], (DEPTH, HEAD_DIM), 0.1),
        'subln_g': 1.0 + _normal(ks[17], (DEPTH, V_DIM), 0.02),
        'w_attn_o': _normal(ks[18], (DEPTH, ATTN_W, d), ATTN_W ** -0.5 * BETA),
        'conv_w': _normal(ks[19], (DEPTH, CONV_K, CONV_W), CONV_K ** -0.5),
        'conv_b': _normal(ks[20], (DEPTH, CONV_W), 0.02),
        'conv_ln_g': 1.0 + _normal(ks[21], (DEPTH, CONV_W), 0.02),
        'conv_ln_b': _normal(ks[22], (DEPTH, CONV_W), 0.02),
        'w_conv_o': _normal(ks[23], (DEPTH, CONV_W, d), CONV_W ** -0.5 * BETA),
        'b_conv_o': _normal(ks[24], (DEPTH, d), 0.02),
        'w_out': _normal(ks[25], (DEPTH, d, d), sd * BETA),
        'ln1_g': 1.0 + _normal(ks[26], (DEPTH, d), 0.02),
        'ln1_b': _normal(ks[27], (DEPTH, d), 0.02),
        'ln2_g': 1.0 + _normal(ks[28], (DEPTH, d), 0.02),
        'ln2_b': _normal(ks[29], (DEPTH, d), 0.02),
        'w_router': _normal(ks[30], (DEPTH, d, e), sd),
        'e_bias': _normal(ks[31], (DEPTH, e), 0.01),
        'w_gate': _normal(ks[32], (DEPTH, e, d, f), sd * BETA),
        'w_up': _normal(ks[33], (DEPTH, e, d, f), sd * BETA),
        'w_down': _normal(ks[34], (DEPTH, e, f, d), f ** -0.5 * BETA),
        'ws_gate': _normal(ks[35], (DEPTH, d, f), sd * BETA),
        'ws_up': _normal(ks[36], (DEPTH, d, f), sd * BETA),
        'ws_down': _normal(ks[37], (DEPTH, f, d), f ** -0.5 * BETA),
    }


def reference(x_prompt, x_sample, cache_k, cache_v, state_conv, page_table, meta_tokens,
              w_in, b_in, lq1, lk1, lq2, lk2, subln_g, w_attn_o, conv_w, conv_b, conv_ln_g, conv_ln_b,
              w_conv_o, b_conv_o, w_out, ln1_g, ln1_b, ln2_g, ln2_b, w_router, e_bias,
              w_gate, w_up, w_down, ws_gate, ws_up, ws_down):
    n_b = x_prompt.shape[0]
    n_db = x_sample.shape[0]
    meta = jnp.broadcast_to(meta_tokens[None].astype(x_prompt.dtype), (n_b, N_META, D_MODEL))
    hp = jnp.concatenate([meta, x_prompt], axis=1)
    hs = x_sample
    kp_l, vp_l, cp_l, ks_l, vs_l, cs_l = [], [], [], [], [], []
    for l in range(DEPTH):
        p = {'w_in': w_in[l], 'b_in': b_in[l], 'lq1': lq1[l], 'lk1': lk1[l], 'lq2': lq2[l], 'lk2': lk2[l],
             'subln_g': subln_g[l], 'w_attn_o': w_attn_o[l], 'conv_w': conv_w[l], 'conv_b': conv_b[l],
             'conv_ln_g': conv_ln_g[l], 'conv_ln_b': conv_ln_b[l], 'w_conv_o': w_conv_o[l],
             'b_conv_o': b_conv_o[l], 'w_out': w_out[l], 'ln1_g': ln1_g[l], 'ln1_b': ln1_b[l],
             'ln2_g': ln2_g[l], 'ln2_b': ln2_b[l], 'w_router': w_router[l], 'e_bias': e_bias[l],
             'w_gate': w_gate[l], 'w_up': w_up[l], 'w_down': w_down[l], 'ws_gate': ws_gate[l],
             'ws_up': ws_up[l], 'ws_down': ws_down[l]}
        lam_init = 0.8 - 0.6 * math.exp(-0.3 * l)
        hp, kp, vp, cp = prompt_layer(hp, p, lam_init)
        k_past = cache_k[l, page_table].reshape(n_db, -1, N_HEADS, 2, HEAD_DIM)
        v_past = cache_v[l, page_table].reshape(n_db, -1, N_HEADS, V_DIM)
        hs, ksn, vsn, csn = sample_layer(hs, k_past, v_past, state_conv[l], p, lam_init)
        kp_l.append(kp)
        vp_l.append(vp)
        cp_l.append(cp)
        ks_l.append(ksn)
        vs_l.append(vsn)
        cs_l.append(csn)
    y_prompt = hp[:, N_META:]
    return (y_prompt, hs, jnp.stack(kp_l), jnp.stack(vp_l), jnp.stack(cp_l),
            jnp.stack(ks_l), jnp.stack(vs_l), jnp.stack(cs_l))
```

```python
import functools
import math

import jax
import jax.numpy as jnp
from jax import lax
from jax.experimental import pallas as pl
from jax.experimental.pallas import tpu as pltpu

F32 = jnp.float32
BF16 = jnp.bfloat16

LN_EPS = 1e-5
NEG = -1e30
N_GROUPS = 8
TOPK_GROUPS = 4
TOP_K = 8
ROUTED_SCALE = 2.5
LANES = 128
VMEM_LIMIT = 48 * 1024 * 1024

_NT = (((1,), (1,)), ((), ()))


def _cparams(*sem):
    return pltpu.CompilerParams(dimension_semantics=sem, vmem_limit_bytes=VMEM_LIMIT)


def _sigmoid(x):
    return 1.0 / (1.0 + jnp.exp(-x))


def _layer_norm(x, g, b):
    mu = jnp.mean(x, axis=-1, keepdims=True)
    xc = x - mu
    var = jnp.mean(xc * xc, axis=-1, keepdims=True)
    return xc * lax.rsqrt(var + LN_EPS) * g + b


def _inproj_kernel(x_ref, w_ref, b_ref, o_ref):
    x = x_ref[...].astype(BF16)
    o_ref[...] = jnp.dot(x, w_ref[...], preferred_element_type=F32) + b_ref[...]


def _inproj(x, w, b, tm, tn):
    m, k = x.shape
    n = w.shape[1]
    return pl.pallas_call(
        _inproj_kernel,
        grid=(m // tm, n // tn),
        in_specs=[pl.BlockSpec((tm, k), lambda i, j: (i, 0)),
                  pl.BlockSpec((k, tn), lambda i, j: (0, j)),
                  pl.BlockSpec((1, tn), lambda i, j: (0, j))],
        out_specs=pl.BlockSpec((tm, tn), lambda i, j: (i, j)),
        out_shape=jax.ShapeDtypeStruct((m, n), F32),
        compiler_params=_cparams("parallel", "parallel"),
        name="inproj",
    )(x, w, b)


def _fold_lanes(x, op):
    out = x[:, :LANES]
    for c in range(1, x.shape[1] // LANES):
        out = op(out, x[:, c * LANES:(c + 1) * LANES])
    return out


def _prompt_attn_kernel(lam_ref, q_ref, k_ref, v_ref, km_ref, vm_ref, g_ref, o_ref,
                        s_ref, sm_ref, red_ref, acc_ref, *, tq, n_meta, half, scale, out_scale):
    qi = pl.program_id(2)
    lam = lam_ref[0]
    q = q_ref[...] * scale
    lane = lax.broadcasted_iota(jnp.int32, (1, LANES), 1)
    first_map = lane < half
    qs = (jnp.where(first_map, q, 0.0).astype(BF16), jnp.where(first_map, 0.0, q).astype(BF16))

    km = km_ref[...].astype(BF16)
    meta_cols = lax.broadcasted_iota(jnp.int32, (tq, LANES), 1) < n_meta
    for mp in range(2):
        s = lax.dot_general(qs[mp], km, _NT, preferred_element_type=F32)
        s = jnp.where(meta_cols, s, NEG)
        sm_ref[mp] = s
        red_ref[mp] = s

    def score_block(j, masked):
        kb = k_ref[pl.ds(pl.multiple_of(j * tq, tq), tq), :].astype(BF16)
        for mp in range(2):
            s = lax.dot_general(qs[mp], kb, _NT, preferred_element_type=F32)
            if masked:
                r = lax.broadcasted_iota(jnp.int32, (tq, tq), 0)
                c = lax.broadcasted_iota(jnp.int32, (tq, tq), 1)
                s = jnp.where(c <= r, s, NEG)
            s_ref[mp, j] = s
            red_ref[mp] = jnp.maximum(red_ref[mp], _fold_lanes(s, jnp.maximum))

    def full_block(j, carry):
        score_block(j, False)
        return carry

    lax.fori_loop(0, qi, full_block, 0)
    score_block(qi, True)
    mx = [jnp.max(red_ref[mp], axis=-1, keepdims=True) for mp in range(2)]

    vm = vm_ref[...].astype(BF16)
    ps = []
    for mp in range(2):
        p = jnp.exp(sm_ref[mp] - mx[mp])
        red_ref[mp] = p
        ps.append(p.astype(BF16))
    acc_ref[...] = jnp.dot(jnp.concatenate(ps, axis=0), vm, preferred_element_type=F32)

    def pv_block(j, carry):
        vb = v_ref[pl.ds(pl.multiple_of(j * tq, tq), tq), :].astype(BF16)
        ps = []
        for mp in range(2):
            p = jnp.exp(s_ref[mp, j] - mx[mp])
            red_ref[mp] = red_ref[mp] + _fold_lanes(p, jnp.add)
            ps.append(p.astype(BF16))
        acc_ref[...] += jnp.dot(jnp.concatenate(ps, axis=0), vb, preferred_element_type=F32)
        return carry

    lax.fori_loop(0, qi + 1, pv_block, 0)
    l1 = jnp.sum(red_ref[0], axis=-1, keepdims=True)
    l2 = jnp.sum(red_ref[1], axis=-1, keepdims=True)
    o = acc_ref[:tq, :] / l1 - lam * (acc_ref[tq:, :] / l2)
    ms = jnp.mean(o * o, axis=-1, keepdims=True)
    o_ref[...] = (o * lax.rsqrt(ms + LN_EPS) * g_ref[...] * out_scale).astype(o_ref.dtype)


def _prompt_attention(z, zm_pad, lam, subln_g, *, n_batch, seq, n_heads, n_meta, half, lam_init, tq):
    nq = seq // tq
    kern = functools.partial(_prompt_attn_kernel, tq=tq, n_meta=n_meta, half=half,
                             scale=half ** -0.5, out_scale=1.0 - lam_init)
    return pl.pallas_call(
        kern,
        grid=(n_batch, n_heads, nq),
        in_specs=[pl.BlockSpec(memory_space=pltpu.SMEM),
                  pl.BlockSpec((tq, LANES), lambda b, h, i: (b * nq + i, h)),
                  pl.BlockSpec((seq, LANES), lambda b, h, i: (b, n_heads + h)),
                  pl.BlockSpec((seq, LANES), lambda b, h, i: (b, 2 * n_heads + h)),
                  pl.BlockSpec((LANES, LANES), lambda b, h, i: (0, n_heads + h)),
                  pl.BlockSpec((LANES, LANES), lambda b, h, i: (0, 2 * n_heads + h)),
                  pl.BlockSpec((1, LANES), lambda b, h, i: (0, 0))],
        out_specs=pl.BlockSpec((tq, LANES), lambda b, h, i: (b * nq + i, h)),
        out_shape=jax.ShapeDtypeStruct((n_batch * seq, n_heads * LANES), BF16),
        scratch_shapes=[pltpu.VMEM((2, nq, tq, tq), F32),
                        pltpu.VMEM((2, tq, LANES), F32),
                        pltpu.VMEM((2, tq, LANES), F32),
                        pltpu.VMEM((2 * tq, LANES), F32)],
        compiler_params=_cparams("parallel", "parallel", "arbitrary"),
        name="prompt_attn",
    )(lam, z, z, z, zm_pad, zm_pad, subln_g)


HIST = 32
CONV_ROWS = 64


def _prompt_conv_kernel(c1_ref, c2_ref, m1_ref, m2_ref, w_ref, cb_ref, g_ref, b_ref,
                        y_ref, tail_ref, ubuf, ybuf, *, tt, n_meta, n_taps):
    t = pl.program_id(1)
    off = HIST - (n_taps - 1)

    @pl.when(t == 0)
    def _():
        ubuf[0:HIST - n_meta, :] = jnp.zeros((HIST - n_meta, ubuf.shape[1]), F32)
        ubuf[HIST - n_meta:HIST, :] = m1_ref[...] * _sigmoid(m2_ref[...])

    @pl.when(t > 0)
    def _():
        ubuf[0:HIST, :] = ubuf[tt:tt + HIST, :]

    ubuf[HIST:HIST + tt, :] = c1_ref[0] * _sigmoid(c2_ref[0])

    def lane_chunk(c, carry):
        cols = pl.ds(pl.multiple_of(c * LANES, LANES), LANES)
        for r0 in range(0, tt, CONV_ROWS):
            acc = jnp.zeros((CONV_ROWS, LANES), F32)
            for j in range(n_taps):
                acc = acc + w_ref[j:j + 1, cols] * ubuf[off + r0 + j:off + r0 + j + CONV_ROWS, cols]
            ybuf[r0:r0 + CONV_ROWS, cols] = acc
        return carry

    lax.fori_loop(0, ubuf.shape[1] // LANES, lane_chunk, 0)
    y = _layer_norm(ybuf[...] + cb_ref[...], g_ref[...], b_ref[...])
    y_ref[0] = (y * _sigmoid(y)).astype(y_ref.dtype)

    @pl.when(t == pl.num_programs(1) - 1)
    def _():
        tail_ref[0] = ubuf[HIST + tt - (n_taps - 1):HIST + tt, :]


def _prompt_conv(z3, zm, conv_w, conv_b, ln_g, ln_b, *, c_blk, tt):
    n_batch, seq, _ = z3.shape
    n_taps, cw = conv_w.shape
    n_meta = zm.shape[0]
    kern = functools.partial(_prompt_conv_kernel, tt=tt, n_meta=n_meta, n_taps=n_taps)
    row = lambda a: a.reshape(1, cw)
    return pl.pallas_call(
        kern,
        grid=(n_batch, seq // tt),
        in_specs=[pl.BlockSpec((1, tt, cw), lambda b, t: (b, t, c_blk)),
                  pl.BlockSpec((1, tt, cw), lambda b, t: (b, t, c_blk + 1)),
                  pl.BlockSpec((n_meta, cw), lambda b, t: (0, c_blk)),
                  pl.BlockSpec((n_meta, cw), lambda b, t: (0, c_blk + 1)),
                  pl.BlockSpec((n_taps, cw), lambda b, t: (0, 0)),
                  pl.BlockSpec((1, cw), lambda b, t: (0, 0)),
                  pl.BlockSpec((1, cw), lambda b, t: (0, 0)),
                  pl.BlockSpec((1, cw), lambda b, t: (0, 0))],
        out_specs=[pl.BlockSpec((1, tt, cw), lambda b, t: (b, t, 0)),
                   pl.BlockSpec((1, n_taps - 1, cw), lambda b, t: (b, 0, 0))],
        out_shape=[jax.ShapeDtypeStruct((n_batch, seq, cw), BF16),
                   jax.ShapeDtypeStruct((n_batch, n_taps - 1, cw), F32)],
        scratch_shapes=[pltpu.VMEM((HIST + tt, cw), F32), pltpu.VMEM((tt, cw), F32)],
        compiler_params=_cparams("parallel", "arbitrary"),
        name="prompt_conv",
    )(z3, z3, zm, zm, conv_w, row(conv_b), row(ln_g), row(ln_b))


def _sample_conv_kernel(c1_ref, c2_ref, st_ref, w_ref, cb_ref, g_ref, b_ref, y_ref, tail_ref, ext,
                        *, t_new, n_taps):
    hist = n_taps - 1
    ext[0:hist, :] = st_ref[0]
    ext[hist:hist + t_new, :] = c1_ref[0] * _sigmoid(c2_ref[0])
    acc = jnp.zeros((t_new, ext.shape[1]), F32)
    for j in range(n_taps):
        acc = acc + w_ref[j:j + 1, :] * ext[j:j + t_new, :]
    y = _layer_norm(acc + cb_ref[...], g_ref[...], b_ref[...])
    y_ref[0] = (y * _sigmoid(y)).astype(y_ref.dtype)
    tail_ref[0] = ext[t_new:t_new + hist, :]


def _sample_conv(z3, state, conv_w, conv_b, ln_g, ln_b, *, c_blk):
    n_batch, t_new, _ = z3.shape
    n_taps, cw = conv_w.shape
    kern = functools.partial(_sample_conv_kernel, t_new=t_new, n_taps=n_taps)
    row = lambda a: a.reshape(1, cw)
    return pl.pallas_call(
        kern,
        grid=(n_batch,),
        in_specs=[pl.BlockSpec((1, t_new, cw), lambda b: (b, 0, c_blk)),
                  pl.BlockSpec((1, t_new, cw), lambda b: (b, 0, c_blk + 1)),
                  pl.BlockSpec((1, n_taps - 1, cw), lambda b: (b, 0, 0)),
                  pl.BlockSpec((n_taps, cw), lambda b: (0, 0)),
                  pl.BlockSpec((1, cw), lambda b: (0, 0)),
                  pl.BlockSpec((1, cw), lambda b: (0, 0)),
                  pl.BlockSpec((1, cw), lambda b: (0, 0))],
        out_specs=[pl.BlockSpec((1, t_new, cw), lambda b: (b, 0, 0)),
                   pl.BlockSpec((1, n_taps - 1, cw), lambda b: (b, 0, 0))],
        out_shape=[jax.ShapeDtypeStruct((n_batch, t_new, cw), F32),
                   jax.ShapeDtypeStruct((n_batch, n_taps - 1, cw), F32)],
        scratch_shapes=[pltpu.VMEM((n_taps - 1 + t_new + 6, cw), F32)],
        compiler_params=_cparams("parallel"),
        name="sample_conv",
    )(z3, z3, state, conv_w, row(conv_b), row(ln_g), row(ln_b))


def _sample_attn_kernel(pt_ref, lam_ref, q_ref, k_ref, v_ref, kn_ref, vn_ref, g_ref, o_ref,
                        m_ref, l_ref, acc_ref, *, t_new, n_heads, out_scale):
    p = pl.program_id(1)
    lam = lam_ref[0]

    @pl.when(p == 0)
    def _():
        m_ref[...] = jnp.full(m_ref.shape, NEG, F32)
        l_ref[...] = jnp.zeros(l_ref.shape, F32)
        acc_ref[...] = jnp.zeros(acc_ref.shape, F32)

    def update(kblk, vblk, mask):
        s = lax.dot_general(q_ref[0], kblk.astype(BF16), _NT, preferred_element_type=F32)
        if mask is not None:
            s = jnp.where(mask, s, NEG)
        m_old = m_ref[...]
        m_new = jnp.maximum(m_old, jnp.max(s, axis=-1, keepdims=True))
        alpha = jnp.exp(m_old - m_new)
        pe = jnp.exp(s - m_new)
        l_ref[...] = alpha * l_ref[...] + jnp.sum(pe, axis=-1, keepdims=True)
        acc_ref[...] = alpha * acc_ref[...] + jnp.dot(pe.astype(BF16), vblk.astype(BF16),
                                                      preferred_element_type=F32)
        m_ref[...] = m_new

    update(k_ref[0], v_ref[0], None)

    @pl.when(p == pl.num_programs(1) - 1)
    def _():
        rows = 2 * t_new * n_heads
        r = lax.broadcasted_iota(jnp.int32, (rows, LANES), 0)
        c = lax.broadcasted_iota(jnp.int32, (rows, LANES), 1)
        update(kn_ref[0], vn_ref[0], c <= r % t_new)
        for h in range(n_heads):
            r1 = h * 2 * t_new
            r2 = r1 + t_new
            cols = slice(h * LANES, (h + 1) * LANES)
            o1 = acc_ref[r1:r1 + t_new, cols] / l_ref[r1:r1 + t_new, :]
            o2 = acc_ref[r2:r2 + t_new, cols] / l_ref[r2:r2 + t_new, :]
            o = o1 - lam * o2
            ms = jnp.mean(o * o, axis=-1, keepdims=True)
            o_ref[0, :, cols] = o * lax.rsqrt(ms + LN_EPS) * g_ref[...] * out_scale


def _sample_attention(page_table, lam, qbd, cache_k, cache_v, knew, vnew, subln_g, *, t_new, n_heads, lam_init):
    n_batch, n_pages = page_table.shape
    page, width = cache_k.shape[1:]
    rows = qbd.shape[1]
    kern = functools.partial(_sample_attn_kernel, t_new=t_new, n_heads=n_heads, out_scale=1.0 - lam_init)
    grid_spec = pltpu.PrefetchScalarGridSpec(
        num_scalar_prefetch=1,
        grid=(n_batch, n_pages),
        in_specs=[pl.BlockSpec(memory_space=pltpu.SMEM),
                  pl.BlockSpec((1, rows, width), lambda b, p, pt: (b, 0, 0)),
                  pl.BlockSpec((1, page, width), lambda b, p, pt: (pt[b, p], 0, 0)),
                  pl.BlockSpec((1, page, width), lambda b, p, pt: (pt[b, p], 0, 0)),
                  pl.BlockSpec((1, LANES, width), lambda b, p, pt: (b, 0, 0)),
                  pl.BlockSpec((1, LANES, width), lambda b, p, pt: (b, 0, 0)),
                  pl.BlockSpec((1, LANES), lambda b, p, pt: (0, 0))],
        out_specs=pl.BlockSpec((1, t_new, width), lambda b, p, pt: (b, 0, 0)),
        scratch_shapes=[pltpu.VMEM((rows, 1), F32), pltpu.VMEM((rows, 1), F32),
                        pltpu.VMEM((rows, width), F32)],
    )
    return pl.pallas_call(
        kern,
        grid_spec=grid_spec,
        out_shape=jax.ShapeDtypeStruct((n_batch, t_new, width), F32),
        compiler_params=_cparams("parallel", "arbitrary"),
        name="sample_attn",
    )(page_table, lam, qbd, cache_k, cache_v, knew, vnew, subln_g)


def _mix_router_kernel(a_ref, y_ref, g1_ref, g2_ref, x_ref, wa_ref, wc_ref, bc_ref, wo_ref,
                       lg_ref, lb_ref, wrh_ref, wrl_ref, eb_ref, h_ref, gate_ref, *, alpha, n_exp):
    att = jnp.dot(a_ref[...].astype(BF16), wa_ref[...], preferred_element_type=F32)
    cnv = jnp.dot(y_ref[...].astype(BF16), wc_ref[...], preferred_element_type=F32) + bc_ref[...]
    mix = _sigmoid(g1_ref[...]) * att + _sigmoid(g2_ref[...]) * cnv
    res = alpha * x_ref[...] + jnp.dot(mix.astype(BF16), wo_ref[...], preferred_element_type=F32)
    h = _layer_norm(res, lg_ref[...], lb_ref[...])
    h_ref[...] = h

    h_hi = h.astype(BF16)
    h_lo = (h - h_hi.astype(F32)).astype(BF16)
    logits = (lax.dot_general(wrh_ref[...], h_hi, _NT, preferred_element_type=F32)
              + lax.dot_general(wrh_ref[...], h_lo, _NT, preferred_element_type=F32)
              + lax.dot_general(wrl_ref[...], h_hi, _NT, preferred_element_type=F32))
    tm = logits.shape[1]
    s = _sigmoid(logits)
    sc = s + eb_ref[...]
    gsz = n_exp // N_GROUPS

    scg = sc.reshape(N_GROUPS, gsz, tm)
    within = lax.broadcasted_iota(jnp.int32, scg.shape, 1)
    m1 = jnp.max(scg, axis=1, keepdims=True)
    first = jnp.min(jnp.where(scg == m1, within, gsz), axis=1, keepdims=True)
    m2 = jnp.max(jnp.where(within == first, -jnp.inf, scg), axis=1, keepdims=True)
    gs = (m1 + m2).reshape(N_GROUPS, tm)

    def rank_rows(x):
        n = x.shape[0]
        idx = lax.broadcasted_iota(jnp.int32, x.shape, 0)
        cnt = jnp.zeros(x.shape, F32)
        for j in range(n):
            row = x[j:j + 1, :]
            cnt = cnt + jnp.where(row > x, 1.0, jnp.where(row == x, (idx > j).astype(F32), 0.0))
        return cnt

    gsel = rank_rows(gs) < TOPK_GROUPS
    emask = jnp.broadcast_to(gsel.astype(F32).reshape(N_GROUPS, 1, tm), scg.shape).reshape(n_exp, tm)
    scm = jnp.where(emask > 0.5, sc, NEG)
    sel = rank_rows(scm) < TOP_K
    w = jnp.where(sel, s, 0.0)
    gate_t = w / jnp.sum(w, axis=0, keepdims=True) * ROUTED_SCALE
    extra = (lax.broadcasted_iota(jnp.int32, (LANES - n_exp, tm), 0) == 0).astype(F32)
    gate_ref[...] = jnp.concatenate([gate_t, extra], axis=0).T


def _mix_router(a, yact, z, x, w_attn_o, w_conv_o, b_conv_o, w_out, ln_g, ln_b, wr_hi, wr_lo, e_bias,
                *, g_blk, alpha, tm):
    t, d = x.shape
    n_exp = wr_hi.shape[0]
    kern = functools.partial(_mix_router_kernel, alpha=alpha, n_exp=n_exp)
    row = lambda v: v.reshape(1, d)
    tile = lambda c: pl.BlockSpec((tm, d), lambda i: (i, c))
    full = lambda shape: pl.BlockSpec(shape, lambda i: (0,) * len(shape))
    return pl.pallas_call(
        kern,
        grid=(t // tm,),
        in_specs=[tile(0), tile(0), tile(g_blk), tile(g_blk + 1), tile(0),
                  full((d, d)), full((d, d)), full((1, d)), full((d, d)), full((1, d)), full((1, d)),
                  full((n_exp, d)), full((n_exp, d)), full((n_exp, 1))],
        out_specs=[tile(0), pl.BlockSpec((tm, LANES), lambda i: (i, 0))],
        out_shape=[jax.ShapeDtypeStruct((t, d), F32), jax.ShapeDtypeStruct((t, LANES), F32)],
        compiler_params=_cparams("parallel"),
        name="mix_router",
    )(a, yact, z, z, x, w_attn_o, w_conv_o, row(b_conv_o), w_out, row(ln_g), row(ln_b),
      wr_hi, wr_lo, e_bias.reshape(n_exp, 1))


def _moe_kernel(h_ref, gate_ref, wgu_ref, wd_ref, lg_ref, lb_ref, o_ref, xb_ref, acc_ref, *, alpha, d_exp):
    e = pl.program_id(1)

    @pl.when(e == 0)
    def _():
        xb_ref[...] = h_ref[...].astype(BF16)
        acc_ref[...] = jnp.zeros(acc_ref.shape, F32)

    gu = jnp.dot(xb_ref[...], wgu_ref[0], preferred_element_type=F32)
    gt = gu[:, :d_exp]
    hid = gt * _sigmoid(gt) * gu[:, d_exp:]
    lane = lax.broadcasted_iota(jnp.int32, gate_ref.shape, 1)
    gcol = jnp.sum(jnp.where(lane == e, gate_ref[...], 0.0), axis=-1, keepdims=True)
    acc_ref[...] += jnp.dot((hid * gcol).astype(BF16), wd_ref[0], preferred_element_type=F32)

    @pl.when(e == pl.num_programs(1) - 1)
    def _():
        o_ref[...] = _layer_norm(alpha * h_ref[...] + acc_ref[...], lg_ref[...], lb_ref[...])


def _moe(h, gate, w_gu, w_d, ln_g, ln_b, *, alpha, tm):
    t, d = h.shape
    n_all, _, two_f = w_gu.shape
    kern = functools.partial(_moe_kernel, alpha=alpha, d_exp=two_f // 2)
    return pl.pallas_call(
        kern,
        grid=(t // tm, n_all),
        in_specs=[pl.BlockSpec((tm, d), lambda i, e: (i, 0)),
                  pl.BlockSpec((tm, LANES), lambda i, e: (i, 0)),
                  pl.BlockSpec((1, d, two_f), lambda i, e: (e, 0, 0)),
                  pl.BlockSpec((1, two_f // 2, d), lambda i, e: (e, 0, 0)),
                  pl.BlockSpec((1, d), lambda i, e: (0, 0)),
                  pl.BlockSpec((1, d), lambda i, e: (0, 0))],
        out_specs=pl.BlockSpec((tm, d), lambda i, e: (i, 0)),
        out_shape=jax.ShapeDtypeStruct((t, d), F32),
        scratch_shapes=[pltpu.VMEM((tm, d), BF16), pltpu.VMEM((tm, d), F32)],
        compiler_params=_cparams("parallel", "arbitrary"),
        name="moe",
    )(h, gate, w_gu, w_d, ln_g.reshape(1, d), ln_b.reshape(1, d))


def kernel(x_prompt, x_sample, cache_k, cache_v, state_conv, page_table, meta_tokens, w_in, b_in, lq1, lk1, lq2, lk2, subln_g, w_attn_o, conv_w, conv_b, conv_ln_g, conv_ln_b, w_conv_o, b_conv_o, w_out, ln1_g, ln1_b, ln2_g, ln2_b, w_router, e_bias, w_gate, w_up, w_down, ws_gate, ws_up, ws_down):
    depth = w_in.shape[0]
    assert depth == 1, "single-layer step"
    n_batch, seq, d = x_prompt.shape
    n_dec, t_new, _ = x_sample.shape
    _, n_pool, page, n_heads, n_maps, half = cache_k.shape
    n_meta = meta_tokens.shape[0]
    n_exp = w_router.shape[-1]
    qk_w = n_heads * n_maps * half
    v_w = cache_v.shape[-1] * n_heads
    cw = conv_w.shape[-1]
    in_w = w_in.shape[-1]
    assert n_maps == 2 and 2 * half == LANES and cache_v.shape[-1] == LANES
    assert qk_w == d and v_w == d and cw == d and in_w == 7 * d
    assert n_meta <= HIST and n_meta % 8 == 0 and conv_w.shape[1] - 1 <= HIST
    c_blk, g_blk = 3, 5
    alpha = (2.0 * depth) ** 0.25
    lam_init = 0.8 - 0.6 * math.exp(-0.3 * 0)
    f32 = lambda a: a.astype(F32)
    lam = (jnp.exp(jnp.sum(f32(lq1[0]) * f32(lk1[0]))) - jnp.exp(jnp.sum(f32(lq2[0]) * f32(lk2[0])))
           + lam_init).reshape(1)

    w_in_b = w_in[0].astype(BF16)
    b_in_r = b_in[0].reshape(1, in_w)
    w_attn_o_b = w_attn_o[0].astype(BF16)
    w_conv_o_b = w_conv_o[0].astype(BF16)
    w_out_b = w_out[0].astype(BF16)
    wr_t = w_router[0].T
    wr_hi = wr_t.astype(BF16)
    wr_lo = (wr_t - wr_hi.astype(F32)).astype(BF16)
    w_gu = jnp.concatenate([jnp.concatenate([w_gate[0], w_up[0]], axis=-1),
                            jnp.concatenate([ws_gate[0], ws_up[0]], axis=-1)[None]], axis=0).astype(BF16)
    w_d = jnp.concatenate([w_down[0], ws_down[0][None]], axis=0).astype(BF16)
    sub_g = subln_g[0].reshape(1, LANES)

    xp = x_prompt.reshape(n_batch * seq, d)
    z = _inproj(xp, w_in_b, b_in_r, 512, d)
    zm = _inproj(meta_tokens.astype(F32), w_in_b, b_in_r, n_meta, d)
    zm_pad = jnp.pad(zm, ((0, LANES - n_meta), (0, 0)))
    a_p = _prompt_attention(z, zm_pad, lam, sub_g, n_batch=n_batch, seq=seq, n_heads=n_heads,
                            n_meta=n_meta, half=half, lam_init=lam_init, tq=512)
    z3 = z.reshape(n_batch, seq, in_w)
    yact_p, conv_prompt = _prompt_conv(z3, zm, conv_w[0], conv_b[0], conv_ln_g[0], conv_ln_b[0],
                                       c_blk=c_blk, tt=256)
    h_p, gate_p = _mix_router(a_p, yact_p.reshape(n_batch * seq, cw), z, xp, w_attn_o_b, w_conv_o_b,
                              b_conv_o[0], w_out_b, ln1_g[0], ln1_b[0], wr_hi, wr_lo, e_bias[0],
                              g_blk=g_blk, alpha=alpha, tm=512)
    y_p = _moe(h_p, gate_p, w_gu, w_d, ln2_g[0], ln2_b[0], alpha=alpha, tm=1024)

    def with_meta(cols, tail):
        m = jnp.broadcast_to(zm[None, :, cols], (n_batch, n_meta, d))
        r = z3[:, :, cols]
        return jnp.concatenate([m, r], axis=1).reshape((1, n_batch, n_meta + seq) + tail)

    k_prompt = with_meta(slice(qk_w, 2 * qk_w), (n_heads, n_maps, half))
    v_prompt = with_meta(slice(2 * qk_w, 2 * qk_w + v_w), (n_heads, LANES))

    xs = x_sample.reshape(n_dec * t_new, d)
    zs = _inproj(xs, w_in_b, b_in_r, n_dec * t_new, d)
    zs3 = zs.reshape(n_dec, t_new, in_w)
    n_hm = n_heads * n_maps
    q_s = (zs3[:, :, :qk_w] * half ** -0.5).reshape(n_dec, t_new, n_hm, half).transpose(0, 2, 1, 3)
    qbd = (q_s[:, :, :, None, :] * jnp.eye(n_hm, dtype=F32)[None, :, None, :, None])
    qbd = qbd.reshape(n_dec, n_hm * t_new, qk_w).astype(BF16)
    pad_new = lambda cols: jnp.pad(zs3[:, :, cols], ((0, 0), (0, LANES - t_new), (0, 0)))
    a_s = _sample_attention(page_table, lam, qbd,
                            cache_k[0].reshape(n_pool, page, qk_w), cache_v[0].reshape(n_pool, page, v_w),
                            pad_new(slice(qk_w, 2 * qk_w)), pad_new(slice(2 * qk_w, 2 * qk_w + v_w)),
                            sub_g, t_new=t_new, n_heads=n_heads, lam_init=lam_init)
    yact_s, conv_sample = _sample_conv(zs3, state_conv[0], conv_w[0], conv_b[0], conv_ln_g[0],
                                       conv_ln_b[0], c_blk=c_blk)
    h_s, gate_s = _mix_router(a_s.reshape(n_dec * t_new, v_w), yact_s.reshape(n_dec * t_new, cw), zs, xs,
                              w_attn_o_b, w_conv_o_b, b_conv_o[0], w_out_b, ln1_g[0], ln1_b[0],
                              wr_hi, wr_lo, e_bias[0], g_blk=g_blk, alpha=alpha, tm=n_dec * t_new)
    y_s = _moe(h_s, gate_s, w_gu, w_d, ln2_g[0], ln2_b[0], alpha=alpha, tm=n_dec * t_new)

    k_sample = zs3[:, :, qk_w:2 * qk_w].reshape(1, n_dec, t_new, n_heads, n_maps, half)
    v_sample = zs3[:, :, 2 * qk_w:2 * qk_w + v_w].reshape(1, n_dec, t_new, n_heads, LANES)
    return (y_p.reshape(n_batch, seq, d), y_s.reshape(n_dec, t_new, d), k_prompt, v_prompt,
            conv_prompt[None], k_sample, v_sample, conv_sample[None])
```

```python
import functools
import math

import jax
import jax.numpy as jnp
from jax import lax
from jax.experimental import pallas as pl
from jax.experimental.pallas import tpu as pltpu

F32 = jnp.float32
BF16 = jnp.bfloat16

LN_EPS = 1e-5
NEG = -1e30
N_GROUPS = 8
TOPK_GROUPS = 4
TOP_K = 8
ROUTED_SCALE = 2.5
LANES = 128
SUBLANES = 8
VMEM_LIMIT = 48 * 1024 * 1024

_NT = (((1,), (1,)), ((), ()))


def _cparams(*sem):
    return pltpu.CompilerParams(dimension_semantics=sem, vmem_limit_bytes=VMEM_LIMIT)


def _sigmoid(x):
    return 1.0 / (1.0 + jnp.exp(-x))


def _layer_norm(x, g, b):
    mu = jnp.mean(x, axis=-1, keepdims=True)
    xc = x - mu
    var = jnp.mean(xc * xc, axis=-1, keepdims=True)
    return xc * lax.rsqrt(var + LN_EPS) * g + b


def _inproj_kernel(x_ref, w_ref, b_ref, o_ref):
    x = x_ref[...].astype(BF16)
    o_ref[...] = jnp.dot(x, w_ref[...], preferred_element_type=F32) + b_ref[...]


def _inproj(x, w, b, tm, tn):
    m, k = x.shape
    n = w.shape[1]
    return pl.pallas_call(
        _inproj_kernel,
        grid=(m // tm, n // tn),
        in_specs=[pl.BlockSpec((tm, k), lambda i, j: (i, 0)),
                  pl.BlockSpec((k, tn), lambda i, j: (0, j)),
                  pl.BlockSpec((1, tn), lambda i, j: (0, j))],
        out_specs=pl.BlockSpec((tm, tn), lambda i, j: (i, j)),
        out_shape=jax.ShapeDtypeStruct((m, n), F32),
        compiler_params=_cparams("parallel", "parallel"),
        name="inproj",
    )(x, w, b)


def _fold_lanes(x, op):
    out = x[:, :LANES]
    for c in range(1, x.shape[1] // LANES):
        out = op(out, x[:, c * LANES:(c + 1) * LANES])
    return out


def _prompt_attn_kernel(lam_ref, q_ref, k_ref, v_ref, km_ref, vm_ref, g_ref, o_ref, ko_ref, vo_ref,
                        s_ref, sm_ref, red_ref, acc_ref, *, tq, n_meta, half, scale, out_scale):
    qi = pl.program_id(2)
    lam = lam_ref[0]

    @pl.when(qi == 0)
    def _():
        ko_ref[0, :n_meta, :] = km_ref[:n_meta, :]
        ko_ref[0, n_meta:, :] = k_ref[...]
        vo_ref[0, :n_meta, :] = vm_ref[:n_meta, :]
        vo_ref[0, n_meta:, :] = v_ref[...]

    q = q_ref[...] * (scale * math.log2(math.e))
    lane = lax.broadcasted_iota(jnp.int32, (1, LANES), 1)
    first_map = lane < half
    qs = (jnp.where(first_map, q, 0.0).astype(BF16), jnp.where(first_map, 0.0, q).astype(BF16))

    km = km_ref[...].astype(BF16)
    meta_cols = lax.broadcasted_iota(jnp.int32, (tq, LANES), 1) < n_meta
    for mp in range(2):
        s = lax.dot_general(qs[mp], km, _NT, preferred_element_type=F32)
        s = jnp.where(meta_cols, s, NEG)
        sm_ref[mp] = s
        red_ref[mp] = s

    def score_block(j, masked):
        kb = k_ref[pl.ds(pl.multiple_of(j * tq, tq), tq), :].astype(BF16)
        for mp in range(2):
            s = lax.dot_general(qs[mp], kb, _NT, preferred_element_type=F32)
            if masked:
                r = lax.broadcasted_iota(jnp.int32, (tq, tq), 0)
                c = lax.broadcasted_iota(jnp.int32, (tq, tq), 1)
                s = jnp.where(c <= r, s, NEG)
            s_ref[mp, j] = s
            red_ref[mp] = jnp.maximum(red_ref[mp], _fold_lanes(s, jnp.maximum))

    def full_block(j, carry):
        score_block(j, False)
        return carry

    lax.fori_loop(0, qi, full_block, 0)
    score_block(qi, True)
    mx = [jnp.max(red_ref[mp], axis=-1, keepdims=True) for mp in range(2)]

    def with_ones(v):
        return jnp.concatenate([v.astype(BF16), jnp.ones(v.shape, BF16)], axis=1)

    def weights(load):
        return jnp.concatenate([jnp.exp2(load(mp) - mx[mp]).astype(BF16) for mp in range(2)], axis=0)

    acc_ref[...] = jnp.dot(weights(lambda mp: sm_ref[mp]), with_ones(vm_ref[...]),
                           preferred_element_type=F32)

    def pv_block(j, carry):
        vb = v_ref[pl.ds(pl.multiple_of(j * tq, tq), tq), :]
        acc_ref[...] += jnp.dot(weights(lambda mp: s_ref[mp, j]), with_ones(vb),
                                preferred_element_type=F32)
        return carry

    lax.fori_loop(0, qi + 1, pv_block, 0)
    o = (acc_ref[:tq, :LANES] / acc_ref[:tq, LANES:]
         - lam * (acc_ref[tq:, :LANES] / acc_ref[tq:, LANES:]))
    ms = jnp.mean(o * o, axis=-1, keepdims=True)
    o_ref[...] = (o * lax.rsqrt(ms + LN_EPS) * g_ref[...] * out_scale).astype(o_ref.dtype)


def _prompt_attention(z, zm_pad, lam, subln_g, *, n_batch, seq, n_heads, n_meta, half, lam_init, tq):
    nq = seq // tq
    kern = functools.partial(_prompt_attn_kernel, tq=tq, n_meta=n_meta, half=half,
                             scale=half ** -0.5, out_scale=1.0 - lam_init)
    kv_out = pl.BlockSpec((1, n_meta + seq, LANES), lambda b, h, i: (b, 0, h))
    kv_shape = jax.ShapeDtypeStruct((n_batch, n_meta + seq, n_heads * LANES), F32)
    return pl.pallas_call(
        kern,
        grid=(n_batch, n_heads, nq),
        in_specs=[pl.BlockSpec(memory_space=pltpu.SMEM),
                  pl.BlockSpec((tq, LANES), lambda b, h, i: (b * nq + i, h)),
                  pl.BlockSpec((seq, LANES), lambda b, h, i: (b, n_heads + h)),
                  pl.BlockSpec((seq, LANES), lambda b, h, i: (b, 2 * n_heads + h)),
                  pl.BlockSpec((LANES, LANES), lambda b, h, i: (0, n_heads + h)),
                  pl.BlockSpec((LANES, LANES), lambda b, h, i: (0, 2 * n_heads + h)),
                  pl.BlockSpec((1, LANES), lambda b, h, i: (0, 0))],
        out_specs=[pl.BlockSpec((tq, LANES), lambda b, h, i: (b * nq + i, h)), kv_out, kv_out],
        out_shape=[jax.ShapeDtypeStruct((n_batch * seq, n_heads * LANES), BF16), kv_shape, kv_shape],
        scratch_shapes=[pltpu.VMEM((2, nq, tq, tq), F32),
                        pltpu.VMEM((2, tq, LANES), F32),
                        pltpu.VMEM((2, tq, LANES), F32),
                        pltpu.VMEM((2 * tq, 2 * LANES), F32)],
        compiler_params=_cparams("parallel", "parallel", "arbitrary"),
        name="prompt_attn",
    )(lam, z, z, z, zm_pad, zm_pad, subln_g)


HIST = 32
CONV_ROWS = 128


def _prompt_conv_kernel(c1_ref, c2_ref, m1_ref, m2_ref, w_ref, cb_ref, g_ref, b_ref,
                        y_ref, tail_ref, ubuf, ybuf, *, tt, n_meta, n_taps):
    t = pl.program_id(1)
    off = HIST - (n_taps - 1)

    @pl.when(t == 0)
    def _():
        ubuf[0:HIST - n_meta, :] = jnp.zeros((HIST - n_meta, ubuf.shape[1]), F32)
        ubuf[HIST - n_meta:HIST, :] = m1_ref[...] * _sigmoid(m2_ref[...])

    @pl.when(t > 0)
    def _():
        ubuf[0:HIST, :] = ubuf[tt:tt + HIST, :]

    ubuf[HIST:HIST + tt, :] = c1_ref[0] * _sigmoid(c2_ref[0])

    def lane_chunk(c, carry):
        cols = pl.ds(pl.multiple_of(c * LANES, LANES), LANES)
        for r0 in range(0, tt, CONV_ROWS):
            acc = None
            for shift in range(SUBLANES):
                part = None
                for j in range(n_taps):
                    if (off + j) % SUBLANES != shift:
                        continue
                    a = off + r0 + j - shift
                    rows = CONV_ROWS + (SUBLANES if shift else 0)
                    term = w_ref[j:j + 1, cols] * ubuf[a:a + rows, cols]
                    part = term if part is None else part + term
                if part is None:
                    continue
                part = part[shift:shift + CONV_ROWS]
                acc = part if acc is None else acc + part
            ybuf[r0:r0 + CONV_ROWS, cols] = acc
        return carry

    lax.fori_loop(0, ubuf.shape[1] // LANES, lane_chunk, 0)
    y = _layer_norm(ybuf[...] + cb_ref[...], g_ref[...], b_ref[...])
    y_ref[0] = (y * _sigmoid(y)).astype(y_ref.dtype)

    @pl.when(t == pl.num_programs(1) - 1)
    def _():
        tail_ref[0] = ubuf[HIST + tt - (n_taps - 1):HIST + tt, :]


def _prompt_conv(z3, zm, conv_w, conv_b, ln_g, ln_b, *, c_blk, tt):
    n_batch, seq, _ = z3.shape
    n_taps, cw = conv_w.shape
    n_meta = zm.shape[0]
    kern = functools.partial(_prompt_conv_kernel, tt=tt, n_meta=n_meta, n_taps=n_taps)
    row = lambda a: a.reshape(1, cw)
    return pl.pallas_call(
        kern,
        grid=(n_batch, seq // tt),
        in_specs=[pl.BlockSpec((1, tt, cw), lambda b, t: (b, t, c_blk)),
                  pl.BlockSpec((1, tt, cw), lambda b, t: (b, t, c_blk + 1)),
                  pl.BlockSpec((n_meta, cw), lambda b, t: (0, c_blk)),
                  pl.BlockSpec((n_meta, cw), lambda b, t: (0, c_blk + 1)),
                  pl.BlockSpec((n_taps, cw), lambda b, t: (0, 0)),
                  pl.BlockSpec((1, cw), lambda b, t: (0, 0)),
                  pl.BlockSpec((1, cw), lambda b, t: (0, 0)),
                  pl.BlockSpec((1, cw), lambda b, t: (0, 0))],
        out_specs=[pl.BlockSpec((1, tt, cw), lambda b, t: (b, t, 0)),
                   pl.BlockSpec((1, n_taps - 1, cw), lambda b, t: (b, 0, 0))],
        out_shape=[jax.ShapeDtypeStruct((n_batch, seq, cw), BF16),
                   jax.ShapeDtypeStruct((n_batch, n_taps - 1, cw), F32)],
        scratch_shapes=[pltpu.VMEM((HIST + tt, cw), F32), pltpu.VMEM((tt, cw), F32)],
        compiler_params=_cparams("parallel", "arbitrary"),
        name="prompt_conv",
    )(z3, z3, zm, zm, conv_w, row(conv_b), row(ln_g), row(ln_b))


def _sample_conv_kernel(c1_ref, c2_ref, st_ref, w_ref, cb_ref, g_ref, b_ref, y_ref, tail_ref, ext,
                        *, t_new, n_taps):
    hist = n_taps - 1
    ext[0:hist, :] = st_ref[0]
    ext[hist:hist + t_new, :] = c1_ref[0] * _sigmoid(c2_ref[0])
    acc = jnp.zeros((t_new, ext.shape[1]), F32)
    for j in range(n_taps):
        acc = acc + w_ref[j:j + 1, :] * ext[j:j + t_new, :]
    y = _layer_norm(acc + cb_ref[...], g_ref[...], b_ref[...])
    y_ref[0] = (y * _sigmoid(y)).astype(y_ref.dtype)
    tail_ref[0] = ext[t_new:t_new + hist, :]


def _sample_conv(z3, state, conv_w, conv_b, ln_g, ln_b, *, c_blk):
    n_batch, t_new, _ = z3.shape
    n_taps, cw = conv_w.shape
    kern = functools.partial(_sample_conv_kernel, t_new=t_new, n_taps=n_taps)
    row = lambda a: a.reshape(1, cw)
    return pl.pallas_call(
        kern,
        grid=(n_batch,),
        in_specs=[pl.BlockSpec((1, t_new, cw), lambda b: (b, 0, c_blk)),
                  pl.BlockSpec((1, t_new, cw), lambda b: (b, 0, c_blk + 1)),
                  pl.BlockSpec((1, n_taps - 1, cw), lambda b: (b, 0, 0)),
                  pl.BlockSpec((n_taps, cw), lambda b: (0, 0)),
                  pl.BlockSpec((1, cw), lambda b: (0, 0)),
                  pl.BlockSpec((1, cw), lambda b: (0, 0)),
                  pl.BlockSpec((1, cw), lambda b: (0, 0))],
        out_specs=[pl.BlockSpec((1, t_new, cw), lambda b: (b, 0, 0)),
                   pl.BlockSpec((1, n_taps - 1, cw), lambda b: (b, 0, 0))],
        out_shape=[jax.ShapeDtypeStruct((n_batch, t_new, cw), F32),
                   jax.ShapeDtypeStruct((n_batch, n_taps - 1, cw), F32)],
        scratch_shapes=[pltpu.VMEM((n_taps - 1 + t_new + 6, cw), F32)],
        compiler_params=_cparams("parallel"),
        name="sample_conv",
    )(z3, z3, state, conv_w, row(conv_b), row(ln_g), row(ln_b))


PAGES_PER_STEP = 8


def _sample_attn_kernel(pt_ref, lam_ref, q_ref, *refs, n_pg, page, t_new, n_heads, out_scale):
    k_refs = refs[:n_pg]
    v_refs = refs[n_pg:2 * n_pg]
    kn_ref, vn_ref, g_ref, o_ref, m_ref, l_ref, acc_ref = refs[2 * n_pg:]
    p = pl.program_id(1)
    lam = lam_ref[0]
    grp = 2 * t_new

    @pl.when(p == 0)
    def _():
        m_ref[...] = jnp.full(m_ref.shape, NEG, F32)
        l_ref[...] = jnp.zeros(l_ref.shape, F32)
        acc_ref[...] = jnp.zeros(acc_ref.shape, F32)

    def update(key_refs, val_refs, mask):
        q = q_ref[0]
        s = jnp.concatenate([jnp.dot(q, kr[0].astype(BF16), preferred_element_type=F32)
                             for kr in key_refs], axis=1)
        if mask is not None:
            s = jnp.where(mask, s, NEG)
        m_old = m_ref[...]
        m_new = jnp.maximum(m_old, jnp.max(s, axis=-1, keepdims=True))
        alpha = jnp.exp(m_old - m_new)
        pe = jnp.exp(s - m_new)
        l_ref[...] = alpha * l_ref[...] + jnp.sum(pe, axis=-1, keepdims=True)
        m_ref[...] = m_new
        for h in range(n_heads):
            rows = slice(h * grp, (h + 1) * grp)
            vh = jnp.concatenate([vr[0, pl.ds(h, page, stride=n_heads), :].astype(BF16)
                                  for vr in val_refs], axis=0)
            acc_ref[rows, :] = alpha[rows] * acc_ref[rows, :] + jnp.dot(
                pe[rows].astype(BF16), vh, preferred_element_type=F32)

    update(k_refs, v_refs, None)

    @pl.when(p == pl.num_programs(1) - 1)
    def _():
        rows = grp * n_heads
        r = lax.broadcasted_iota(jnp.int32, (rows, LANES), 0)
        c = lax.broadcasted_iota(jnp.int32, (rows, LANES), 1)
        update([kn_ref], [vn_ref], c <= r % t_new)
        for h in range(n_heads):
            r1 = h * grp
            r2 = r1 + t_new
            o1 = acc_ref[r1:r1 + t_new, :] / l_ref[r1:r1 + t_new, :]
            o2 = acc_ref[r2:r2 + t_new, :] / l_ref[r2:r2 + t_new, :]
            o = o1 - lam * o2
            ms = jnp.mean(o * o, axis=-1, keepdims=True)
            o_ref[0, :, h * LANES:(h + 1) * LANES] = o * lax.rsqrt(ms + LN_EPS) * g_ref[...] * out_scale


def _sample_attention(page_table, lam, qbd, cache_kt, cache_v, knew_t, vnew, subln_g, *, t_new, lam_init):
    n_batch, n_pages = page_table.shape
    _, width, page = cache_kt.shape
    v_rows, v_dim = cache_v.shape[1:]
    n_heads = v_rows // page
    rows = qbd.shape[1]
    n_pg = PAGES_PER_STEP
    kern = functools.partial(_sample_attn_kernel, n_pg=n_pg, page=page, t_new=t_new, n_heads=n_heads,
                             out_scale=1.0 - lam_init)

    def page_map(i):
        return lambda b, p, pt: (pt[b, p * n_pg + i], 0, 0)

    grid_spec = pltpu.PrefetchScalarGridSpec(
        num_scalar_prefetch=1,
        grid=(n_batch, n_pages // n_pg),
        in_specs=([pl.BlockSpec(memory_space=pltpu.SMEM),
                   pl.BlockSpec((1, rows, width), lambda b, p, pt: (b, 0, 0))]
                  + [pl.BlockSpec((1, width, page), page_map(i)) for i in range(n_pg)]
                  + [pl.BlockSpec((1, v_rows, v_dim), page_map(i)) for i in range(n_pg)]
                  + [pl.BlockSpec((1, width, page), lambda b, p, pt: (b, 0, 0)),
                     pl.BlockSpec((1, v_rows, v_dim), lambda b, p, pt: (b, 0, 0)),
                     pl.BlockSpec((1, v_dim), lambda b, p, pt: (0, 0))]),
        out_specs=pl.BlockSpec((1, t_new, n_heads * v_dim), lambda b, p, pt: (b, 0, 0)),
        scratch_shapes=[pltpu.VMEM((rows, 1), F32), pltpu.VMEM((rows, 1), F32),
                        pltpu.VMEM((rows, v_dim), F32)],
    )
    return pl.pallas_call(
        kern,
        grid_spec=grid_spec,
        out_shape=jax.ShapeDtypeStruct((n_batch, t_new, n_heads * v_dim), F32),
        compiler_params=_cparams("parallel", "arbitrary"),
        name="sample_attn",
    )(page_table, lam, qbd, *([cache_kt] * n_pg), *([cache_v] * n_pg), knew_t, vnew, subln_g)


def _mix_router_kernel(a_ref, y_ref, g1_ref, g2_ref, x_ref, wa_ref, wc_ref, bc_ref, wo_ref,
                       lg_ref, lb_ref, wrh_ref, wrl_ref, eb_ref, h_ref, gate_ref, *, alpha, n_exp):
    att = jnp.dot(a_ref[...].astype(BF16), wa_ref[...], preferred_element_type=F32)
    cnv = jnp.dot(y_ref[...].astype(BF16), wc_ref[...], preferred_element_type=F32) + bc_ref[...]
    mix = _sigmoid(g1_ref[...]) * att + _sigmoid(g2_ref[...]) * cnv
    res = alpha * x_ref[...] + jnp.dot(mix.astype(BF16), wo_ref[...], preferred_element_type=F32)
    h = _layer_norm(res, lg_ref[...], lb_ref[...])
    h_ref[...] = h

    h_hi = h.astype(BF16)
    h_lo = (h - h_hi.astype(F32)).astype(BF16)
    logits = (lax.dot_general(wrh_ref[...], h_hi, _NT, preferred_element_type=F32)
              + lax.dot_general(wrh_ref[...], h_lo, _NT, preferred_element_type=F32)
              + lax.dot_general(wrl_ref[...], h_hi, _NT, preferred_element_type=F32))
    tm = logits.shape[1]
    s = _sigmoid(logits)
    sc = s + eb_ref[...]
    gsz = n_exp // N_GROUPS

    scg = sc.reshape(N_GROUPS, gsz, tm)
    within = lax.broadcasted_iota(jnp.int32, scg.shape, 1)
    m1 = jnp.max(scg, axis=1, keepdims=True)
    first = jnp.min(jnp.where(scg == m1, within, gsz), axis=1, keepdims=True)
    m2 = jnp.max(jnp.where(within == first, -jnp.inf, scg), axis=1, keepdims=True)
    gs = (m1 + m2).reshape(N_GROUPS, tm)

    def rank_rows(x):
        n = x.shape[0]
        idx = lax.broadcasted_iota(jnp.int32, x.shape, 0)
        cnt = jnp.zeros(x.shape, F32)
        for j in range(n):
            row = x[j:j + 1, :]
            cnt = cnt + jnp.where(row > x, 1.0, jnp.where(row == x, (idx > j).astype(F32), 0.0))
        return cnt

    gsel = rank_rows(gs) < TOPK_GROUPS
    emask = jnp.broadcast_to(gsel.astype(F32).reshape(N_GROUPS, 1, tm), scg.shape).reshape(n_exp, tm)
    scm = jnp.where(emask > 0.5, sc, NEG)
    sel = rank_rows(scm) < TOP_K
    w = jnp.where(sel, s, 0.0)
    gate_t = w / jnp.sum(w, axis=0, keepdims=True) * ROUTED_SCALE
    pad = jnp.zeros((LANES - n_exp, tm), F32)
    gate_ref[...] = jnp.concatenate([gate_t, pad], axis=0).T


def _mix_router(a, yact, z, x, w_attn_o, w_conv_o, b_conv_o, w_out, ln_g, ln_b, wr_hi, wr_lo, e_bias,
                *, g_blk, alpha, tm):
    t, d = x.shape
    n_exp = wr_hi.shape[0]
    kern = functools.partial(_mix_router_kernel, alpha=alpha, n_exp=n_exp)
    row = lambda v: v.reshape(1, d)
    tile = lambda c: pl.BlockSpec((tm, d), lambda i: (i, c))
    full = lambda shape: pl.BlockSpec(shape, lambda i: (0,) * len(shape))
    return pl.pallas_call(
        kern,
        grid=(t // tm,),
        in_specs=[tile(0), tile(0), tile(g_blk), tile(g_blk + 1), tile(0),
                  full((d, d)), full((d, d)), full((1, d)), full((d, d)), full((1, d)), full((1, d)),
                  full((n_exp, d)), full((n_exp, d)), full((n_exp, 1))],
        out_specs=[tile(0), pl.BlockSpec((tm, LANES), lambda i: (i, 0))],
        out_shape=[jax.ShapeDtypeStruct((t, d), F32), jax.ShapeDtypeStruct((t, LANES), F32)],
        compiler_params=_cparams("parallel"),
        name="mix_router",
    )(a, yact, z, z, x, w_attn_o, w_conv_o, row(b_conv_o), w_out, row(ln_g), row(ln_b),
      wr_hi, wr_lo, e_bias.reshape(n_exp, 1))


def _moe_kernel(h_ref, gate_ref, wg_ref, wu_ref, wd_ref, sg_ref, su_ref, sd_ref, lg_ref, lb_ref, o_ref,
                xb_ref, acc_ref, *, alpha, n_exp):
    e = pl.program_id(1)

    @pl.when(e == 0)
    def _():
        xb_ref[...] = h_ref[...].astype(BF16)
        acc_ref[...] = jnp.zeros(acc_ref.shape, F32)

    def expert(wg, wu, wd, gcol):
        x = xb_ref[...]
        gt = jnp.dot(x, wg.astype(BF16), preferred_element_type=F32)
        up = jnp.dot(x, wu.astype(BF16), preferred_element_type=F32)
        hid = gt * _sigmoid(gt) * up
        if gcol is not None:
            hid = hid * gcol
        acc_ref[...] += jnp.dot(hid.astype(BF16), wd.astype(BF16), preferred_element_type=F32)

    @pl.when(e < n_exp)
    def _():
        lane = lax.broadcasted_iota(jnp.int32, gate_ref.shape, 1)
        gcol = jnp.sum(jnp.where(lane == e, gate_ref[...], 0.0), axis=-1, keepdims=True)
        expert(wg_ref[0, 0], wu_ref[0, 0], wd_ref[0, 0], gcol)

    @pl.when(e == n_exp)
    def _():
        expert(sg_ref[0], su_ref[0], sd_ref[0], None)
        o_ref[...] = _layer_norm(alpha * h_ref[...] + acc_ref[...], lg_ref[...], lb_ref[...])


def _moe(h, gate, w_gate, w_up, w_down, ws_gate, ws_up, ws_down, ln_g, ln_b, *, alpha, tm):
    t, d = h.shape
    _, n_exp, _, f = w_gate.shape
    kern = functools.partial(_moe_kernel, alpha=alpha, n_exp=n_exp)
    routed = lambda shape: pl.BlockSpec((1, 1) + shape, lambda i, e: (0, jnp.minimum(e, n_exp - 1), 0, 0))
    shared = lambda shape: pl.BlockSpec((1,) + shape, lambda i, e: (0, 0, 0))
    return pl.pallas_call(
        kern,
        grid=(t // tm, n_exp + 1),
        in_specs=[pl.BlockSpec((tm, d), lambda i, e: (i, 0)),
                  pl.BlockSpec((tm, LANES), lambda i, e: (i, 0)),
                  routed((d, f)), routed((d, f)), routed((f, d)),
                  shared((d, f)), shared((d, f)), shared((f, d)),
                  pl.BlockSpec((1, d), lambda i, e: (0, 0)),
                  pl.BlockSpec((1, d), lambda i, e: (0, 0))],
        out_specs=pl.BlockSpec((tm, d), lambda i, e: (i, 0)),
        out_shape=jax.ShapeDtypeStruct((t, d), F32),
        scratch_shapes=[pltpu.VMEM((tm, d), BF16), pltpu.VMEM((tm, d), F32)],
        compiler_params=_cparams("parallel", "arbitrary"),
        name="moe",
    )(h, gate, w_gate, w_up, w_down, ws_gate, ws_up, ws_down, ln_g.reshape(1, d), ln_b.reshape(1, d))


def kernel(x_prompt, x_sample, cache_k, cache_v, state_conv, page_table, meta_tokens, w_in, b_in, lq1, lk1, lq2, lk2, subln_g, w_attn_o, conv_w, conv_b, conv_ln_g, conv_ln_b, w_conv_o, b_conv_o, w_out, ln1_g, ln1_b, ln2_g, ln2_b, w_router, e_bias, w_gate, w_up, w_down, ws_gate, ws_up, ws_down):
    depth = w_in.shape[0]
    assert depth == 1, "single-layer step"
    n_batch, seq, d = x_prompt.shape
    n_dec, t_new, _ = x_sample.shape
    _, n_pool, page, n_heads, n_maps, half = cache_k.shape
    n_meta = meta_tokens.shape[0]
    qk_w = n_heads * n_maps * half
    v_w = cache_v.shape[-1] * n_heads
    cw = conv_w.shape[-1]
    in_w = w_in.shape[-1]
    assert n_maps == 2 and 2 * half == LANES and cache_v.shape[-1] == LANES and page == LANES
    assert qk_w == d and v_w == d and cw == d and in_w == 7 * d
    assert n_meta <= HIST and n_meta % 8 == 0 and conv_w.shape[1] - 1 <= HIST
    assert page_table.shape[1] % PAGES_PER_STEP == 0
    c_blk, g_blk = 3, 5
    alpha = (2.0 * depth) ** 0.25
    lam_init = 0.8 - 0.6 * math.exp(-0.3 * 0)
    f32 = lambda a: a.astype(F32)
    lam = (jnp.exp(jnp.sum(f32(lq1[0]) * f32(lk1[0]))) - jnp.exp(jnp.sum(f32(lq2[0]) * f32(lk2[0])))
           + lam_init).reshape(1)

    w_in_b = w_in[0].astype(BF16)
    b_in_r = b_in[0].reshape(1, in_w)
    w_attn_o_b = w_attn_o[0].astype(BF16)
    w_conv_o_b = w_conv_o[0].astype(BF16)
    w_out_b = w_out[0].astype(BF16)
    wr_t = w_router[0].T
    wr_hi = wr_t.astype(BF16)
    wr_lo = (wr_t - wr_hi.astype(F32)).astype(BF16)
    sub_g = subln_g[0].reshape(1, LANES)
    moe = functools.partial(_moe, w_gate=w_gate, w_up=w_up, w_down=w_down, ws_gate=ws_gate, ws_up=ws_up,
                            ws_down=ws_down, ln_g=ln2_g[0], ln_b=ln2_b[0], alpha=alpha)

    xp = x_prompt.reshape(n_batch * seq, d)
    z = _inproj(xp, w_in_b, b_in_r, min(2048, n_batch * seq), 512)
    zm = _inproj(meta_tokens.astype(F32), w_in_b, b_in_r, n_meta, d)
    zm_pad = jnp.pad(zm, ((0, LANES - n_meta), (0, 0)))
    a_p, k_all, v_all = _prompt_attention(z, zm_pad, lam, sub_g, n_batch=n_batch, seq=seq, n_heads=n_heads,
                                          n_meta=n_meta, half=half, lam_init=lam_init, tq=512)
    z3 = z.reshape(n_batch, seq, in_w)
    yact_p, conv_prompt = _prompt_conv(z3, zm, conv_w[0], conv_b[0], conv_ln_g[0], conv_ln_b[0],
                                       c_blk=c_blk, tt=256)
    h_p, gate_p = _mix_router(a_p, yact_p.reshape(n_batch * seq, cw), z, xp, w_attn_o_b, w_conv_o_b,
                              b_conv_o[0], w_out_b, ln1_g[0], ln1_b[0], wr_hi, wr_lo, e_bias[0],
                              g_blk=g_blk, alpha=alpha, tm=512)
    y_p = moe(h_p, gate_p, tm=1024)
    k_prompt = k_all.reshape(1, n_batch, n_meta + seq, n_heads, n_maps, half)
    v_prompt = v_all.reshape(1, n_batch, n_meta + seq, n_heads, LANES)

    xs = x_sample.reshape(n_dec * t_new, d)
    zs = _inproj(xs, w_in_b, b_in_r, n_dec * t_new, d)
    zs3 = zs.reshape(n_dec, t_new, in_w)
    n_hm = n_heads * n_maps
    q_rep = jnp.tile(zs3[:, :, :qk_w] * half ** -0.5, (1, n_hm, 1))
    own = (jnp.arange(n_hm * t_new)[:, None] // t_new) == (jnp.arange(qk_w)[None, :] // half)
    qbd = jnp.where(own[None], q_rep, 0.0).astype(BF16)
    k_new = zs3[:, :, qk_w:2 * qk_w]
    v_new = zs3[:, :, 2 * qk_w:2 * qk_w + v_w]
    knew_t = jnp.pad(k_new.transpose(0, 2, 1), ((0, 0), (0, 0), (0, page - t_new)))
    vnew_p = jnp.pad(v_new.reshape(n_dec, t_new * n_heads, LANES), ((0, 0), (0, (page - t_new) * n_heads), (0, 0)))
    cache_kt = jnp.transpose(cache_k[0], (0, 2, 3, 4, 1)).reshape(n_pool, qk_w, page)
    cache_vr = cache_v[0].reshape(n_pool, page * n_heads, LANES)
    a_s = _sample_attention(page_table, lam, qbd, cache_kt, cache_vr, knew_t, vnew_p, sub_g,
                            t_new=t_new, lam_init=lam_init)
    yact_s, conv_sample = _sample_conv(zs3, state_conv[0], conv_w[0], conv_b[0], conv_ln_g[0],
                                       conv_ln_b[0], c_blk=c_blk)
    h_s, gate_s = _mix_router(a_s.reshape(n_dec * t_new, v_w), yact_s.reshape(n_dec * t_new, cw), zs, xs,
                              w_attn_o_b, w_conv_o_b, b_conv_o[0], w_out_b, ln1_g[0], ln1_b[0],
                              wr_hi, wr_lo, e_bias[0], g_blk=g_blk, alpha=alpha, tm=n_dec * t_new)
    y_s = moe(h_s, gate_s, tm=n_dec * t_new)

    k_sample = k_new.reshape(1, n_dec, t_new, n_heads, n_maps, half)
    v_sample = v_new.reshape(1, n_dec, t_new, n_heads, LANES)
    return (y_p.reshape(n_batch, seq, d), y_s.reshape(n_dec, t_new, d), k_prompt, v_prompt,
            conv_prompt[None], k_sample, v_sample, conv_sample[None])
```

```python
import functools
import math

import jax
import jax.numpy as jnp
from jax import lax
from jax.experimental import pallas as pl
from jax.experimental.pallas import tpu as pltpu

F32 = jnp.float32
BF16 = jnp.bfloat16

LN_EPS = 1e-5
NEG = -1e30
N_GROUPS = 8
TOPK_GROUPS = 4
TOP_K = 8
ROUTED_SCALE = 2.5
LANES = 128
SUBLANES = 8
VMEM_LIMIT = 48 * 1024 * 1024

_NT = (((1,), (1,)), ((), ()))


def _cparams(*sem):
    return pltpu.CompilerParams(dimension_semantics=sem, vmem_limit_bytes=VMEM_LIMIT)


def _sigmoid(x):
    return 1.0 / (1.0 + jnp.exp(-x))


def _layer_norm(x, g, b):
    mu = jnp.mean(x, axis=-1, keepdims=True)
    xc = x - mu
    var = jnp.mean(xc * xc, axis=-1, keepdims=True)
    return xc * lax.rsqrt(var + LN_EPS) * g + b


def _inproj_kernel(x_ref, w_ref, b_ref, o_ref):
    x = x_ref[...].astype(BF16)
    o_ref[...] = jnp.dot(x, w_ref[...], preferred_element_type=F32) + b_ref[...]


def _inproj(x, w, b, tm, tn):
    m, k = x.shape
    n = w.shape[1]
    return pl.pallas_call(
        _inproj_kernel,
        grid=(m // tm, n // tn),
        in_specs=[pl.BlockSpec((tm, k), lambda i, j: (i, 0)),
                  pl.BlockSpec((k, tn), lambda i, j: (0, j)),
                  pl.BlockSpec((1, tn), lambda i, j: (0, j))],
        out_specs=pl.BlockSpec((tm, tn), lambda i, j: (i, j)),
        out_shape=jax.ShapeDtypeStruct((m, n), F32),
        compiler_params=_cparams("parallel", "parallel"),
        name="inproj",
    )(x, w, b)


def _fold_lanes(x, op):
    out = x[:, :LANES]
    for c in range(1, x.shape[1] // LANES):
        out = op(out, x[:, c * LANES:(c + 1) * LANES])
    return out


def _prompt_attn_kernel(lam_ref, q_ref, k_ref, v_ref, km_ref, vm_ref, g_ref, o_ref, ko_ref, vo_ref,
                        s_ref, sm_ref, red_ref, acc_ref, *, tq, n_meta, half, scale, out_scale):
    qi = pl.program_id(2)
    lam = lam_ref[0]

    @pl.when(qi == 0)
    def _():
        ko_ref[0, :n_meta, :] = km_ref[:n_meta, :]
        ko_ref[0, n_meta:, :] = k_ref[...]
        vo_ref[0, :n_meta, :] = vm_ref[:n_meta, :]
        vo_ref[0, n_meta:, :] = v_ref[...]

    q = q_ref[...] * (scale * math.log2(math.e))
    lane = lax.broadcasted_iota(jnp.int32, (1, LANES), 1)
    first_map = lane < half
    qs = (jnp.where(first_map, q, 0.0).astype(BF16), jnp.where(first_map, 0.0, q).astype(BF16))

    km = km_ref[...].astype(BF16)
    meta_cols = lax.broadcasted_iota(jnp.int32, (tq, LANES), 1) < n_meta
    for mp in range(2):
        s = lax.dot_general(qs[mp], km, _NT, preferred_element_type=F32)
        s = jnp.where(meta_cols, s, NEG)
        sm_ref[mp] = s
        red_ref[mp] = s

    def score_block(j, masked):
        kb = k_ref[pl.ds(pl.multiple_of(j * tq, tq), tq), :].astype(BF16)
        for mp in range(2):
            s = lax.dot_general(qs[mp], kb, _NT, preferred_element_type=F32)
            if masked:
                r = lax.broadcasted_iota(jnp.int32, (tq, tq), 0)
                c = lax.broadcasted_iota(jnp.int32, (tq, tq), 1)
                s = jnp.where(c <= r, s, NEG)
            s_ref[mp, j] = s
            red_ref[mp] = jnp.maximum(red_ref[mp], _fold_lanes(s, jnp.maximum))

    def full_block(j, carry):
        score_block(j, False)
        return carry

    lax.fori_loop(0, qi, full_block, 0)
    score_block(qi, True)
    mx = [jnp.max(red_ref[mp], axis=-1, keepdims=True) for mp in range(2)]

    def with_ones(v):
        return jnp.concatenate([v.astype(BF16), jnp.ones(v.shape, BF16)], axis=1)

    def weights(load):
        return jnp.concatenate([jnp.exp2(load(mp) - mx[mp]).astype(BF16) for mp in range(2)], axis=0)

    acc_ref[...] = jnp.dot(weights(lambda mp: sm_ref[mp]), with_ones(vm_ref[...]),
                           preferred_element_type=F32)

    def pv_block(j, carry):
        vb = v_ref[pl.ds(pl.multiple_of(j * tq, tq), tq), :]
        acc_ref[...] += jnp.dot(weights(lambda mp: s_ref[mp, j]), with_ones(vb),
                                preferred_element_type=F32)
        return carry

    lax.fori_loop(0, qi + 1, pv_block, 0)
    o = (acc_ref[:tq, :LANES] / acc_ref[:tq, LANES:]
         - lam * (acc_ref[tq:, :LANES] / acc_ref[tq:, LANES:]))
    ms = jnp.mean(o * o, axis=-1, keepdims=True)
    o_ref[...] = (o * lax.rsqrt(ms + LN_EPS) * g_ref[...] * out_scale).astype(o_ref.dtype)


def _prompt_attention(z, zm_pad, lam, subln_g, *, n_batch, seq, n_heads, n_meta, half, lam_init, tq):
    nq = seq // tq
    kern = functools.partial(_prompt_attn_kernel, tq=tq, n_meta=n_meta, half=half,
                             scale=half ** -0.5, out_scale=1.0 - lam_init)
    kv_out = pl.BlockSpec((1, n_meta + seq, LANES), lambda b, h, i: (b, 0, h))
    kv_shape = jax.ShapeDtypeStruct((n_batch, n_meta + seq, n_heads * LANES), F32)
    return pl.pallas_call(
        kern,
        grid=(n_batch, n_heads, nq),
        in_specs=[pl.BlockSpec(memory_space=pltpu.SMEM),
                  pl.BlockSpec((tq, LANES), lambda b, h, i: (b * nq + i, h)),
                  pl.BlockSpec((seq, LANES), lambda b, h, i: (b, n_heads + h)),
                  pl.BlockSpec((seq, LANES), lambda b, h, i: (b, 2 * n_heads + h)),
                  pl.BlockSpec((LANES, LANES), lambda b, h, i: (0, n_heads + h)),
                  pl.BlockSpec((LANES, LANES), lambda b, h, i: (0, 2 * n_heads + h)),
                  pl.BlockSpec((1, LANES), lambda b, h, i: (0, 0))],
        out_specs=[pl.BlockSpec((tq, LANES), lambda b, h, i: (b * nq + i, h)), kv_out, kv_out],
        out_shape=[jax.ShapeDtypeStruct((n_batch * seq, n_heads * LANES), BF16), kv_shape, kv_shape],
        scratch_shapes=[pltpu.VMEM((2, nq, tq, tq), F32),
                        pltpu.VMEM((2, tq, LANES), F32),
                        pltpu.VMEM((2, tq, LANES), F32),
                        pltpu.VMEM((2 * tq, 2 * LANES), F32)],
        compiler_params=_cparams("parallel", "parallel", "arbitrary"),
        name="prompt_attn",
    )(lam, z, z, z, zm_pad, zm_pad, subln_g)


HIST = 32
CONV_ROWS = 128


def _prompt_conv_kernel(c1_ref, c2_ref, m1_ref, m2_ref, w_ref, cb_ref, g_ref, b_ref,
                        y_ref, tail_ref, ubuf, ybuf, *, tt, n_meta, n_taps):
    t = pl.program_id(1)
    off = HIST - (n_taps - 1)

    @pl.when(t == 0)
    def _():
        ubuf[0:HIST - n_meta, :] = jnp.zeros((HIST - n_meta, ubuf.shape[1]), F32)
        ubuf[HIST - n_meta:HIST, :] = m1_ref[...] * _sigmoid(m2_ref[...])

    @pl.when(t > 0)
    def _():
        ubuf[0:HIST, :] = ubuf[tt:tt + HIST, :]

    ubuf[HIST:HIST + tt, :] = c1_ref[0] * _sigmoid(c2_ref[0])

    def lane_chunk(c, carry):
        cols = pl.ds(pl.multiple_of(c * LANES, LANES), LANES)
        for r0 in range(0, tt, CONV_ROWS):
            acc = None
            for shift in range(SUBLANES):
                part = None
                for j in range(n_taps):
                    if (off + j) % SUBLANES != shift:
                        continue
                    a = off + r0 + j - shift
                    rows = CONV_ROWS + (SUBLANES if shift else 0)
                    term = w_ref[j:j + 1, cols] * ubuf[a:a + rows, cols]
                    part = term if part is None else part + term
                if part is None:
                    continue
                part = part[shift:shift + CONV_ROWS]
                acc = part if acc is None else acc + part
            ybuf[r0:r0 + CONV_ROWS, cols] = acc
        return carry

    lax.fori_loop(0, ubuf.shape[1] // LANES, lane_chunk, 0)
    y = _layer_norm(ybuf[...] + cb_ref[...], g_ref[...], b_ref[...])
    y_ref[0] = (y * _sigmoid(y)).astype(y_ref.dtype)

    @pl.when(t == pl.num_programs(1) - 1)
    def _():
        tail_ref[0] = ubuf[HIST + tt - (n_taps - 1):HIST + tt, :]


def _prompt_conv(z3, zm, conv_w, conv_b, ln_g, ln_b, *, c_blk, tt):
    n_batch, seq, _ = z3.shape
    n_taps, cw = conv_w.shape
    n_meta = zm.shape[0]
    kern = functools.partial(_prompt_conv_kernel, tt=tt, n_meta=n_meta, n_taps=n_taps)
    row = lambda a: a.reshape(1, cw)
    return pl.pallas_call(
        kern,
        grid=(n_batch, seq // tt),
        in_specs=[pl.BlockSpec((1, tt, cw), lambda b, t: (b, t, c_blk)),
                  pl.BlockSpec((1, tt, cw), lambda b, t: (b, t, c_blk + 1)),
                  pl.BlockSpec((n_meta, cw), lambda b, t: (0, c_blk)),
                  pl.BlockSpec((n_meta, cw), lambda b, t: (0, c_blk + 1)),
                  pl.BlockSpec((n_taps, cw), lambda b, t: (0, 0)),
                  pl.BlockSpec((1, cw), lambda b, t: (0, 0)),
                  pl.BlockSpec((1, cw), lambda b, t: (0, 0)),
                  pl.BlockSpec((1, cw), lambda b, t: (0, 0))],
        out_specs=[pl.BlockSpec((1, tt, cw), lambda b, t: (b, t, 0)),
                   pl.BlockSpec((1, n_taps - 1, cw), lambda b, t: (b, 0, 0))],
        out_shape=[jax.ShapeDtypeStruct((n_batch, seq, cw), BF16),
                   jax.ShapeDtypeStruct((n_batch, n_taps - 1, cw), F32)],
        scratch_shapes=[pltpu.VMEM((HIST + tt, cw), F32), pltpu.VMEM((tt, cw), F32)],
        compiler_params=_cparams("parallel", "arbitrary"),
        name="prompt_conv",
    )(z3, z3, zm, zm, conv_w, row(conv_b), row(ln_g), row(ln_b))


def _sample_conv_kernel(c1_ref, c2_ref, st_ref, w_ref, cb_ref, g_ref, b_ref, y_ref, tail_ref, ext,
                        *, t_new, n_taps):
    hist = n_taps - 1
    ext[0:hist, :] = st_ref[0]
    ext[hist:hist + t_new, :] = c1_ref[0] * _sigmoid(c2_ref[0])
    acc = jnp.zeros((t_new, ext.shape[1]), F32)
    for j in range(n_taps):
        acc = acc + w_ref[j:j + 1, :] * ext[j:j + t_new, :]
    y = _layer_norm(acc + cb_ref[...], g_ref[...], b_ref[...])
    y_ref[0] = (y * _sigmoid(y)).astype(y_ref.dtype)
    tail_ref[0] = ext[t_new:t_new + hist, :]


def _sample_conv(z3, state, conv_w, conv_b, ln_g, ln_b, *, c_blk):
    n_batch, t_new, _ = z3.shape
    n_taps, cw = conv_w.shape
    kern = functools.partial(_sample_conv_kernel, t_new=t_new, n_taps=n_taps)
    row = lambda a: a.reshape(1, cw)
    return pl.pallas_call(
        kern,
        grid=(n_batch,),
        in_specs=[pl.BlockSpec((1, t_new, cw), lambda b: (b, 0, c_blk)),
                  pl.BlockSpec((1, t_new, cw), lambda b: (b, 0, c_blk + 1)),
                  pl.BlockSpec((1, n_taps - 1, cw), lambda b: (b, 0, 0)),
                  pl.BlockSpec((n_taps, cw), lambda b: (0, 0)),
                  pl.BlockSpec((1, cw), lambda b: (0, 0)),
                  pl.BlockSpec((1, cw), lambda b: (0, 0)),
                  pl.BlockSpec((1, cw), lambda b: (0, 0))],
        out_specs=[pl.BlockSpec((1, t_new, cw), lambda b: (b, 0, 0)),
                   pl.BlockSpec((1, n_taps - 1, cw), lambda b: (b, 0, 0))],
        out_shape=[jax.ShapeDtypeStruct((n_batch, t_new, cw), F32),
                   jax.ShapeDtypeStruct((n_batch, n_taps - 1, cw), F32)],
        scratch_shapes=[pltpu.VMEM((n_taps - 1 + t_new + 6, cw), F32)],
        compiler_params=_cparams("parallel"),
        name="sample_conv",
    )(z3, z3, state, conv_w, row(conv_b), row(ln_g), row(ln_b))


PAGES_PER_STEP = 4


class _PagedAttn:
    def __init__(self, refs, scratch, *, page, t_new, n_heads, out_scale):
        n_pg = PAGES_PER_STEP
        self.lam_ref, self.q_ref = refs[:2]
        self.k_refs = refs[2:2 + n_pg]
        self.v_refs = refs[2 + n_pg:2 + 2 * n_pg]
        self.kn_ref, self.vn_ref, self.g_ref, self.o_ref = refs[2 + 2 * n_pg:]
        self.m_ref, self.l_ref, self.acc_ref = scratch
        self.page, self.t_new, self.n_heads, self.out_scale = page, t_new, n_heads, out_scale
        self.grp = 2 * t_new

    def update(self, key_refs, val_refs, *, first=None, commit=None, mask=None):
        q = self.q_ref[0]
        s = jnp.concatenate([jnp.dot(q, kr[0].astype(BF16), preferred_element_type=F32)
                             for kr in key_refs], axis=1)
        if mask is not None:
            s = jnp.where(mask, s, NEG)
        restart = (lambda ref, v: ref[...]) if first is None else (lambda ref, v: jnp.where(first, v, ref[...]))
        keep = (lambda new, old: new) if commit is None else (lambda new, old: jnp.where(commit, new, old))
        m_old = restart(self.m_ref, NEG)
        m_new = jnp.maximum(m_old, jnp.max(s, axis=-1, keepdims=True))
        alpha = jnp.exp(m_old - m_new)
        pe = jnp.exp(s - m_new)
        l_new = alpha * restart(self.l_ref, 0.0) + jnp.sum(pe, axis=-1, keepdims=True)
        self.l_ref[...] = keep(l_new, self.l_ref[...])
        self.m_ref[...] = keep(m_new, self.m_ref[...])
        for h in range(self.n_heads):
            rows = slice(h * self.grp, (h + 1) * self.grp)
            vh = jnp.concatenate([vr[0, pl.ds(h, self.page, stride=self.n_heads), :].astype(BF16)
                                  for vr in val_refs], axis=0)
            acc_old = self.acc_ref[rows, :]
            if first is not None:
                acc_old = jnp.where(first, 0.0, acc_old)
            acc_new = alpha[rows] * acc_old + jnp.dot(pe[rows].astype(BF16), vh, preferred_element_type=F32)
            self.acc_ref[rows, :] = keep(acc_new, self.acc_ref[rows, :])

    def pages(self, *, first, commit):
        self.update(self.k_refs, self.v_refs, first=first, commit=commit)

    def finish(self):
        t_new, grp = self.t_new, self.grp
        rows = grp * self.n_heads
        r = lax.broadcasted_iota(jnp.int32, (rows, LANES), 0)
        c = lax.broadcasted_iota(jnp.int32, (rows, LANES), 1)
        self.update([self.kn_ref], [self.vn_ref], mask=c <= r % t_new)
        lam = self.lam_ref[0]
        for h in range(self.n_heads):
            r1 = h * grp
            r2 = r1 + t_new
            o1 = self.acc_ref[r1:r1 + t_new, :] / self.l_ref[r1:r1 + t_new, :]
            o2 = self.acc_ref[r2:r2 + t_new, :] / self.l_ref[r2:r2 + t_new, :]
            o = o1 - lam * o2
            ms = jnp.mean(o * o, axis=-1, keepdims=True)
            self.o_ref[0, :, h * LANES:(h + 1) * LANES] = (
                o * lax.rsqrt(ms + LN_EPS) * self.g_ref[...] * self.out_scale)


def _mix_router_kernel(a_ref, y_ref, g1_ref, g2_ref, x_ref, wa_ref, wc_ref, bc_ref, wo_ref,
                       lg_ref, lb_ref, wrh_ref, wrl_ref, eb_ref, h_ref, gate_ref, *, alpha, n_exp):
    att = jnp.dot(a_ref[...].astype(BF16), wa_ref[...], preferred_element_type=F32)
    cnv = jnp.dot(y_ref[...].astype(BF16), wc_ref[...], preferred_element_type=F32) + bc_ref[...]
    mix = _sigmoid(g1_ref[...]) * att + _sigmoid(g2_ref[...]) * cnv
    res = alpha * x_ref[...] + jnp.dot(mix.astype(BF16), wo_ref[...], preferred_element_type=F32)
    h = _layer_norm(res, lg_ref[...], lb_ref[...])
    h_ref[...] = h

    h_hi = h.astype(BF16)
    h_lo = (h - h_hi.astype(F32)).astype(BF16)
    logits = (lax.dot_general(wrh_ref[...], h_hi, _NT, preferred_element_type=F32)
              + lax.dot_general(wrh_ref[...], h_lo, _NT, preferred_element_type=F32)
              + lax.dot_general(wrl_ref[...], h_hi, _NT, preferred_element_type=F32))
    tm = logits.shape[1]
    s = _sigmoid(logits)
    sc = s + eb_ref[...]
    gsz = n_exp // N_GROUPS

    scg = sc.reshape(N_GROUPS, gsz, tm)
    within = lax.broadcasted_iota(jnp.int32, scg.shape, 1)
    m1 = jnp.max(scg, axis=1, keepdims=True)
    first = jnp.min(jnp.where(scg == m1, within, gsz), axis=1, keepdims=True)
    m2 = jnp.max(jnp.where(within == first, -jnp.inf, scg), axis=1, keepdims=True)
    gs = (m1 + m2).reshape(N_GROUPS, tm)

    def rank_rows(x):
        n = x.shape[0]
        idx = lax.broadcasted_iota(jnp.int32, x.shape, 0)
        cnt = jnp.zeros(x.shape, F32)
        for j in range(n):
            row = x[j:j + 1, :]
            cnt = cnt + jnp.where(row > x, 1.0, jnp.where(row == x, (idx > j).astype(F32), 0.0))
        return cnt

    gsel = rank_rows(gs) < TOPK_GROUPS
    emask = jnp.broadcast_to(gsel.astype(F32).reshape(N_GROUPS, 1, tm), scg.shape).reshape(n_exp, tm)
    scm = jnp.where(emask > 0.5, sc, NEG)
    sel = rank_rows(scm) < TOP_K
    w = jnp.where(sel, s, 0.0)
    gate_t = w / jnp.sum(w, axis=0, keepdims=True) * ROUTED_SCALE
    pad = jnp.zeros((LANES - n_exp, tm), F32)
    gate_ref[...] = jnp.concatenate([gate_t, pad], axis=0).T


def _mix_router(a, yact, z, x, w_attn_o, w_conv_o, b_conv_o, w_out, ln_g, ln_b, wr_hi, wr_lo, e_bias,
                *, g_blk, alpha, tm):
    t, d = x.shape
    n_exp = wr_hi.shape[0]
    kern = functools.partial(_mix_router_kernel, alpha=alpha, n_exp=n_exp)
    row = lambda v: v.reshape(1, d)
    tile = lambda c: pl.BlockSpec((tm, d), lambda i: (i, c))
    full = lambda shape: pl.BlockSpec(shape, lambda i: (0,) * len(shape))
    return pl.pallas_call(
        kern,
        grid=(t // tm,),
        in_specs=[tile(0), tile(0), tile(g_blk), tile(g_blk + 1), tile(0),
                  full((d, d)), full((d, d)), full((1, d)), full((d, d)), full((1, d)), full((1, d)),
                  full((n_exp, d)), full((n_exp, d)), full((n_exp, 1))],
        out_specs=[tile(0), pl.BlockSpec((tm, LANES), lambda i: (i, 0))],
        out_shape=[jax.ShapeDtypeStruct((t, d), F32), jax.ShapeDtypeStruct((t, LANES), F32)],
        compiler_params=_cparams("parallel"),
        name="mix_router",
    )(a, yact, z, z, x, w_attn_o, w_conv_o, row(b_conv_o), w_out, row(ln_g), row(ln_b),
      wr_hi, wr_lo, e_bias.reshape(n_exp, 1))


N_MOE_IN = 10
N_ATTN_IN = 2 + 2 * PAGES_PER_STEP + 3


def _moe_kernel(*refs, alpha, n_exp, attn):
    if attn is None:
        moe_in, (o_ref, xb_ref, acc_ref) = refs[:N_MOE_IN], refs[N_MOE_IN:]
        pa = None
    else:
        refs = refs[1:]
        moe_in = refs[:N_MOE_IN]
        attn_in = refs[N_MOE_IN:N_MOE_IN + N_ATTN_IN]
        o_ref, a_ref, xb_ref, acc_ref, m_ref, l_ref, pacc_ref = refs[N_MOE_IN + N_ATTN_IN:]
        pa = _PagedAttn(attn_in + (a_ref,), (m_ref, l_ref, pacc_ref), page=attn["page"],
                        t_new=attn["t_new"], n_heads=attn["n_heads"], out_scale=attn["out_scale"])
    h_ref, gate_ref, wg_ref, wu_ref, wd_ref, sg_ref, su_ref, sd_ref, lg_ref, lb_ref = moe_in
    e = pl.program_id(1)

    @pl.when(e == 0)
    def _():
        xb_ref[...] = h_ref[...].astype(BF16)
        acc_ref[...] = jnp.zeros(acc_ref.shape, F32)

    def attend_pages():
        if pa is None:
            return
        step = pl.program_id(0) * pl.num_programs(1) + e
        pa.pages(first=step % attn["groups_per_seq"] == 0, commit=step < attn["n_groups"])

    def expert(wg, wu, wd, gcol):
        x = xb_ref[...]
        gt = jnp.dot(x, wg, preferred_element_type=F32)
        up = jnp.dot(x, wu, preferred_element_type=F32)
        hid = gt * _sigmoid(gt) * up
        if gcol is not None:
            hid = hid * gcol
        acc_ref[...] += jnp.dot(hid.astype(BF16), wd, preferred_element_type=F32)

    @pl.when(e < n_exp)
    def _():
        lane = lax.broadcasted_iota(jnp.int32, gate_ref.shape, 1)
        gcol = jnp.sum(jnp.where(lane == e, gate_ref[...], 0.0), axis=-1, keepdims=True)
        expert(wg_ref[0], wu_ref[0], wd_ref[0], gcol)
        attend_pages()

    @pl.when(e == n_exp)
    def _():
        expert(sg_ref[...], su_ref[...], sd_ref[...], None)
        attend_pages()
        o_ref[...] = _layer_norm(alpha * h_ref[...] + acc_ref[...], lg_ref[...], lb_ref[...])

    if pa is not None:
        step = pl.program_id(0) * pl.num_programs(1) + e
        last_of_seq = step % attn["groups_per_seq"] == attn["groups_per_seq"] - 1

        @pl.when(jnp.logical_and(step < attn["n_groups"], last_of_seq))
        def _():
            pa.finish()


def _moe(h, gate, w_gate, w_up, w_down, ws_gate, ws_up, ws_down, ln_g, ln_b, *, alpha, tm, paged=None):
    t, d = h.shape
    n_exp, _, f = w_gate.shape
    n_steps = n_exp + 1
    grid = (t // tm, n_steps)
    routed = lambda shape: pl.BlockSpec((1,) + shape, lambda i, e, *_: (jnp.minimum(e, n_exp - 1), 0, 0))
    const = lambda shape: pl.BlockSpec(shape, lambda i, e, *_: (0,) * len(shape))
    tile = lambda w: pl.BlockSpec((tm, w), lambda i, e, *_: (i, 0))
    in_specs = [tile(d), tile(LANES), routed((d, f)), routed((d, f)), routed((f, d)),
                const((d, f)), const((d, f)), const((f, d)), const((1, d)), const((1, d))]
    args = [h, gate, w_gate, w_up, w_down, ws_gate, ws_up, ws_down, ln_g.reshape(1, d), ln_b.reshape(1, d)]
    out_specs = [tile(d)]
    out_shape = [jax.ShapeDtypeStruct((t, d), F32)]
    scratch = [pltpu.VMEM((tm, d), BF16), pltpu.VMEM((tm, d), F32)]
    if paged is None:
        kern = functools.partial(_moe_kernel, alpha=alpha, n_exp=n_exp, attn=None)
        return pl.pallas_call(
            kern, grid=grid, in_specs=in_specs, out_specs=out_specs, out_shape=out_shape,
            scratch_shapes=scratch, compiler_params=_cparams("parallel", "arbitrary"), name="moe",
        )(*args)[0]

    page_table, lam, qbd, cache_kt, cache_v, knew_t, vnew, subln_g, t_new, lam_init = paged
    n_seq, n_pages = page_table.shape
    _, width, page = cache_kt.shape
    v_rows, v_dim = cache_v.shape[1:]
    n_heads = v_rows // page
    rows = qbd.shape[1]
    n_pg = PAGES_PER_STEP
    gps = n_pages // n_pg
    n_groups = n_seq * gps
    assert n_pages % n_pg == 0 and n_groups <= grid[0] * n_steps, "not enough MoE steps to host the pages"

    def seq_and_group(i, e):
        g = jnp.minimum(i * n_steps + e, n_groups - 1)
        return g // gps, g % gps

    def page_map(k):
        def index(i, e, pt):
            b, p = seq_and_group(i, e)
            return (pt[b, p * n_pg + k], 0, 0)
        return index

    per_seq = lambda shape: pl.BlockSpec((1,) + shape, lambda i, e, pt: (seq_and_group(i, e)[0], 0, 0))
    in_specs += ([pl.BlockSpec(memory_space=pltpu.SMEM), per_seq((rows, width))]
                 + [pl.BlockSpec((1, width, page), page_map(k)) for k in range(n_pg)]
                 + [pl.BlockSpec((1, v_rows, v_dim), page_map(k)) for k in range(n_pg)]
                 + [per_seq((width, page)), per_seq((v_rows, v_dim)), const((1, v_dim))])
    args += [lam, qbd] + [cache_kt] * n_pg + [cache_v] * n_pg + [knew_t, vnew, subln_g]
    out_specs.append(per_seq((t_new, n_heads * v_dim)))
    out_shape.append(jax.ShapeDtypeStruct((n_seq, t_new, n_heads * v_dim), F32))
    scratch += [pltpu.VMEM((rows, 1), F32), pltpu.VMEM((rows, 1), F32), pltpu.VMEM((rows, v_dim), F32)]
    attn = dict(page=page, t_new=t_new, n_heads=n_heads, out_scale=1.0 - lam_init,
                n_groups=n_groups, groups_per_seq=gps)
    kern = functools.partial(_moe_kernel, alpha=alpha, n_exp=n_exp, attn=attn)
    return pl.pallas_call(
        kern,
        grid_spec=pltpu.PrefetchScalarGridSpec(num_scalar_prefetch=1, grid=grid, in_specs=in_specs,
                                               out_specs=out_specs, scratch_shapes=scratch),
        out_shape=out_shape,
        compiler_params=_cparams("arbitrary", "arbitrary"),
        name="moe_paged_attn",
    )(page_table, *args)


def kernel(x_prompt, x_sample, cache_k, cache_v, state_conv, page_table, meta_tokens, w_in, b_in, lq1, lk1, lq2, lk2, subln_g, w_attn_o, conv_w, conv_b, conv_ln_g, conv_ln_b, w_conv_o, b_conv_o, w_out, ln1_g, ln1_b, ln2_g, ln2_b, w_router, e_bias, w_gate, w_up, w_down, ws_gate, ws_up, ws_down):
    depth = w_in.shape[0]
    assert depth == 1, "single-layer step"
    n_batch, seq, d = x_prompt.shape
    n_dec, t_new, _ = x_sample.shape
    _, n_pool, page, n_heads, n_maps, half = cache_k.shape
    n_meta = meta_tokens.shape[0]
    qk_w = n_heads * n_maps * half
    v_w = cache_v.shape[-1] * n_heads
    cw = conv_w.shape[-1]
    in_w = w_in.shape[-1]
    assert n_maps == 2 and 2 * half == LANES and cache_v.shape[-1] == LANES and page == LANES
    assert qk_w == d and v_w == d and cw == d and in_w == 7 * d
    assert n_meta <= HIST and n_meta % 8 == 0 and conv_w.shape[1] - 1 <= HIST
    assert page_table.shape[1] % PAGES_PER_STEP == 0
    c_blk, g_blk = 3, 5
    alpha = (2.0 * depth) ** 0.25
    lam_init = 0.8 - 0.6 * math.exp(-0.3 * 0)
    f32 = lambda a: a.astype(F32)
    lam = (jnp.exp(jnp.sum(f32(lq1[0]) * f32(lk1[0]))) - jnp.exp(jnp.sum(f32(lq2[0]) * f32(lk2[0])))
           + lam_init).reshape(1)

    w_in_b = w_in[0].astype(BF16)
    b_in_r = b_in[0].reshape(1, in_w)
    w_attn_o_b = w_attn_o[0].astype(BF16)
    w_conv_o_b = w_conv_o[0].astype(BF16)
    w_out_b = w_out[0].astype(BF16)
    wr_t = w_router[0].T
    wr_hi = wr_t.astype(BF16)
    wr_lo = (wr_t - wr_hi.astype(F32)).astype(BF16)
    sub_g = subln_g[0].reshape(1, LANES)
    bf = lambda w: w[0].astype(BF16)
    moe = functools.partial(_moe, w_gate=bf(w_gate), w_up=bf(w_up), w_down=bf(w_down), ws_gate=bf(ws_gate),
                            ws_up=bf(ws_up), ws_down=bf(ws_down), ln_g=ln2_g[0], ln_b=ln2_b[0], alpha=alpha)

    xs = x_sample.reshape(n_dec * t_new, d)
    zs = _inproj(xs, w_in_b, b_in_r, n_dec * t_new, d)
    zs3 = zs.reshape(n_dec, t_new, in_w)
    n_hm = n_heads * n_maps
    q_rep = jnp.tile(zs3[:, :, :qk_w] * half ** -0.5, (1, n_hm, 1))
    own = (jnp.arange(n_hm * t_new)[:, None] // t_new) == (jnp.arange(qk_w)[None, :] // half)
    qbd = jnp.where(own[None], q_rep, 0.0).astype(BF16)
    k_new = zs3[:, :, qk_w:2 * qk_w]
    v_new = zs3[:, :, 2 * qk_w:2 * qk_w + v_w]
    knew_t = jnp.pad(k_new.transpose(0, 2, 1), ((0, 0), (0, 0), (0, page - t_new)))
    vnew_p = jnp.pad(v_new.reshape(n_dec, t_new * n_heads, LANES), ((0, 0), (0, (page - t_new) * n_heads), (0, 0)))
    cache_kt = jnp.transpose(cache_k[0], (0, 2, 3, 4, 1)).reshape(n_pool, qk_w, page)
    cache_vr = cache_v[0].reshape(n_pool, page * n_heads, LANES)
    paged = (page_table, lam, qbd, cache_kt, cache_vr, knew_t, vnew_p, sub_g, t_new, lam_init)

    xp = x_prompt.reshape(n_batch * seq, d)
    z = _inproj(xp, w_in_b, b_in_r, min(2048, n_batch * seq), 512)
    zm = _inproj(meta_tokens.astype(F32), w_in_b, b_in_r, n_meta, d)
    zm_pad = jnp.pad(zm, ((0, LANES - n_meta), (0, 0)))
    a_p, k_all, v_all = _prompt_attention(z, zm_pad, lam, sub_g, n_batch=n_batch, seq=seq, n_heads=n_heads,
                                          n_meta=n_meta, half=half, lam_init=lam_init, tq=512)
    z3 = z.reshape(n_batch, seq, in_w)
    yact_p, conv_prompt = _prompt_conv(z3, zm, conv_w[0], conv_b[0], conv_ln_g[0], conv_ln_b[0],
                                       c_blk=c_blk, tt=256)
    h_p, gate_p = _mix_router(a_p, yact_p.reshape(n_batch * seq, cw), z, xp, w_attn_o_b, w_conv_o_b,
                              b_conv_o[0], w_out_b, ln1_g[0], ln1_b[0], wr_hi, wr_lo, e_bias[0],
                              g_blk=g_blk, alpha=alpha, tm=512)
    y_p, a_s = moe(h_p, gate_p, tm=min(1024, n_batch * seq), paged=paged)
    k_prompt = k_all.reshape(1, n_batch, n_meta + seq, n_heads, n_maps, half)
    v_prompt = v_all.reshape(1, n_batch, n_meta + seq, n_heads, LANES)

    yact_s, conv_sample = _sample_conv(zs3, state_conv[0], conv_w[0], conv_b[0], conv_ln_g[0],
                                       conv_ln_b[0], c_blk=c_blk)
    h_s, gate_s = _mix_router(a_s.reshape(n_dec * t_new, v_w), yact_s.reshape(n_dec * t_new, cw), zs, xs,
                              w_attn_o_b, w_conv_o_b, b_conv_o[0], w_out_b, ln1_g[0], ln1_b[0],
                              wr_hi, wr_lo, e_bias[0], g_blk=g_blk, alpha=alpha, tm=n_dec * t_new)
    y_s = moe(h_s, gate_s, tm=n_dec * t_new)

    k_sample = k_new.reshape(1, n_dec, t_new, n_heads, n_maps, half)
    v_sample = v_new.reshape(1, n_dec, t_new, n_heads, LANES)
    return (y_p.reshape(n_batch, seq, d), y_s.reshape(n_dec, t_new, d), k_prompt, v_prompt,
            conv_prompt[None], k_sample, v_sample, conv_sample[None])
```

```python
import functools
import math

import jax
import jax.numpy as jnp
from jax import lax
from jax.experimental import pallas as pl
from jax.experimental.pallas import tpu as pltpu

F32 = jnp.float32
BF16 = jnp.bfloat16

LN_EPS = 1e-5
NEG = -1e30
N_GROUPS = 8
TOPK_GROUPS = 4
TOP_K = 8
ROUTED_SCALE = 2.5
LANES = 128
SUBLANES = 8
VMEM_LIMIT = 48 * 1024 * 1024

_NT = (((1,), (1,)), ((), ()))


def _cparams(*sem):
    return pltpu.CompilerParams(dimension_semantics=sem, vmem_limit_bytes=VMEM_LIMIT)


def _sigmoid(x):
    return 1.0 / (1.0 + jnp.exp(-x))


def _layer_norm(x, g, b):
    mu = jnp.mean(x, axis=-1, keepdims=True)
    xc = x - mu
    var = jnp.mean(xc * xc, axis=-1, keepdims=True)
    return xc * lax.rsqrt(var + LN_EPS) * g + b


def _inproj_kernel(x_ref, w_ref, b_ref, o_ref):
    x = x_ref[...].astype(BF16)
    o_ref[...] = jnp.dot(x, w_ref[...], preferred_element_type=F32) + b_ref[...]


def _inproj(x, w, b, tm, tn):
    m, k = x.shape
    n = w.shape[1]
    return pl.pallas_call(
        _inproj_kernel,
        grid=(m // tm, n // tn),
        in_specs=[pl.BlockSpec((tm, k), lambda i, j: (i, 0)),
                  pl.BlockSpec((k, tn), lambda i, j: (0, j)),
                  pl.BlockSpec((1, tn), lambda i, j: (0, j))],
        out_specs=pl.BlockSpec((tm, tn), lambda i, j: (i, j)),
        out_shape=jax.ShapeDtypeStruct((m, n), F32),
        compiler_params=_cparams("parallel", "parallel"),
        name="inproj",
    )(x, w, b)


def _fold_lanes(x, op):
    out = x[:, :LANES]
    for c in range(1, x.shape[1] // LANES):
        out = op(out, x[:, c * LANES:(c + 1) * LANES])
    return out


def _prompt_attn_kernel(lam_ref, q_ref, k_ref, v_ref, km_ref, vm_ref, g_ref, o_ref, ko_ref, vo_ref,
                        s_ref, sm_ref, red_ref, acc_ref, *, tq, n_meta, half, scale, out_scale):
    qi = pl.program_id(2)
    lam = lam_ref[0]

    @pl.when(qi == 0)
    def _():
        ko_ref[0, :n_meta, :] = km_ref[:n_meta, :]
        ko_ref[0, n_meta:, :] = k_ref[...]
        vo_ref[0, :n_meta, :] = vm_ref[:n_meta, :]
        vo_ref[0, n_meta:, :] = v_ref[...]

    q = q_ref[...] * (scale * math.log2(math.e))
    lane = lax.broadcasted_iota(jnp.int32, (1, LANES), 1)
    first_map = lane < half
    qs = (jnp.where(first_map, q, 0.0).astype(BF16), jnp.where(first_map, 0.0, q).astype(BF16))

    km = km_ref[...].astype(BF16)
    meta_cols = lax.broadcasted_iota(jnp.int32, (tq, LANES), 1) < n_meta
    for mp in range(2):
        s = lax.dot_general(qs[mp], km, _NT, preferred_element_type=F32)
        s = jnp.where(meta_cols, s, NEG)
        sm_ref[mp] = s
        red_ref[mp] = s

    def score_block(j, masked):
        kb = k_ref[pl.ds(pl.multiple_of(j * tq, tq), tq), :].astype(BF16)
        for mp in range(2):
            s = lax.dot_general(qs[mp], kb, _NT, preferred_element_type=F32)
            if masked:
                r = lax.broadcasted_iota(jnp.int32, (tq, tq), 0)
                c = lax.broadcasted_iota(jnp.int32, (tq, tq), 1)
                s = jnp.where(c <= r, s, NEG)
            s_ref[mp, j] = s
            red_ref[mp] = jnp.maximum(red_ref[mp], _fold_lanes(s, jnp.maximum))

    def full_block(j, carry):
        score_block(j, False)
        return carry

    lax.fori_loop(0, qi, full_block, 0)
    score_block(qi, True)
    mx = [jnp.max(red_ref[mp], axis=-1, keepdims=True) for mp in range(2)]

    def with_ones(v):
        return jnp.concatenate([v.astype(BF16), jnp.ones(v.shape, BF16)], axis=1)

    def weights(load):
        return jnp.concatenate([jnp.exp2(load(mp) - mx[mp]).astype(BF16) for mp in range(2)], axis=0)

    acc_ref[...] = jnp.dot(weights(lambda mp: sm_ref[mp]), with_ones(vm_ref[...]),
                           preferred_element_type=F32)

    def pv_block(j, carry):
        vb = v_ref[pl.ds(pl.multiple_of(j * tq, tq), tq), :]
        acc_ref[...] += jnp.dot(weights(lambda mp: s_ref[mp, j]), with_ones(vb),
                                preferred_element_type=F32)
        return carry

    lax.fori_loop(0, qi + 1, pv_block, 0)
    o = (acc_ref[:tq, :LANES] / acc_ref[:tq, LANES:]
         - lam * (acc_ref[tq:, :LANES] / acc_ref[tq:, LANES:]))
    ms = jnp.mean(o * o, axis=-1, keepdims=True)
    o_ref[...] = (o * lax.rsqrt(ms + LN_EPS) * g_ref[...] * out_scale).astype(o_ref.dtype)


def _prompt_attention(z, zm_pad, lam, subln_g, *, n_batch, seq, n_heads, n_meta, half, lam_init, tq):
    nq = seq // tq
    kern = functools.partial(_prompt_attn_kernel, tq=tq, n_meta=n_meta, half=half,
                             scale=half ** -0.5, out_scale=1.0 - lam_init)
    kv_out = pl.BlockSpec((1, n_meta + seq, LANES), lambda b, h, i: (b, 0, h))
    kv_shape = jax.ShapeDtypeStruct((n_batch, n_meta + seq, n_heads * LANES), F32)
    return pl.pallas_call(
        kern,
        grid=(n_batch, n_heads, nq),
        in_specs=[pl.BlockSpec(memory_space=pltpu.SMEM),
                  pl.BlockSpec((tq, LANES), lambda b, h, i: (b * nq + i, h)),
                  pl.BlockSpec((seq, LANES), lambda b, h, i: (b, n_heads + h)),
                  pl.BlockSpec((seq, LANES), lambda b, h, i: (b, 2 * n_heads + h)),
                  pl.BlockSpec((LANES, LANES), lambda b, h, i: (0, n_heads + h)),
                  pl.BlockSpec((LANES, LANES), lambda b, h, i: (0, 2 * n_heads + h)),
                  pl.BlockSpec((1, LANES), lambda b, h, i: (0, 0))],
        out_specs=[pl.BlockSpec((tq, LANES), lambda b, h, i: (b * nq + i, h)), kv_out, kv_out],
        out_shape=[jax.ShapeDtypeStruct((n_batch * seq, n_heads * LANES), BF16), kv_shape, kv_shape],
        scratch_shapes=[pltpu.VMEM((2, nq, tq, tq), F32),
                        pltpu.VMEM((2, tq, LANES), F32),
                        pltpu.VMEM((2, tq, LANES), F32),
                        pltpu.VMEM((2 * tq, 2 * LANES), F32)],
        compiler_params=_cparams("parallel", "parallel", "arbitrary"),
        name="prompt_attn",
    )(lam, z, z, z, zm_pad, zm_pad, subln_g)


HIST = 32
CONV_ROWS = 128


def _prompt_conv_kernel(c1_ref, c2_ref, m1_ref, m2_ref, w_ref, cb_ref, g_ref, b_ref,
                        y_ref, tail_ref, ubuf, ybuf, *, tt, n_meta, n_taps):
    t = pl.program_id(1)
    off = HIST - (n_taps - 1)

    @pl.when(t == 0)
    def _():
        ubuf[0:HIST - n_meta, :] = jnp.zeros((HIST - n_meta, ubuf.shape[1]), F32)
        ubuf[HIST - n_meta:HIST, :] = m1_ref[...] * _sigmoid(m2_ref[...])

    @pl.when(t > 0)
    def _():
        ubuf[0:HIST, :] = ubuf[tt:tt + HIST, :]

    ubuf[HIST:HIST + tt, :] = c1_ref[0] * _sigmoid(c2_ref[0])

    def lane_chunk(c, carry):
        cols = pl.ds(pl.multiple_of(c * LANES, LANES), LANES)
        for r0 in range(0, tt, CONV_ROWS):
            acc = None
            for shift in range(SUBLANES):
                part = None
                for j in range(n_taps):
                    if (off + j) % SUBLANES != shift:
                        continue
                    a = off + r0 + j - shift
                    rows = CONV_ROWS + (SUBLANES if shift else 0)
                    term = w_ref[j:j + 1, cols] * ubuf[a:a + rows, cols]
                    part = term if part is None else part + term
                if part is None:
                    continue
                part = part[shift:shift + CONV_ROWS]
                acc = part if acc is None else acc + part
            ybuf[r0:r0 + CONV_ROWS, cols] = acc
        return carry

    lax.fori_loop(0, ubuf.shape[1] // LANES, lane_chunk, 0)
    y = _layer_norm(ybuf[...] + cb_ref[...], g_ref[...], b_ref[...])
    y_ref[0] = (y * _sigmoid(y)).astype(y_ref.dtype)

    @pl.when(t == pl.num_programs(1) - 1)
    def _():
        tail_ref[0] = ubuf[HIST + tt - (n_taps - 1):HIST + tt, :]


def _prompt_conv(z3, zm, conv_w, conv_b, ln_g, ln_b, *, c_blk, tt):
    n_batch, seq, _ = z3.shape
    n_taps, cw = conv_w.shape
    n_meta = zm.shape[0]
    kern = functools.partial(_prompt_conv_kernel, tt=tt, n_meta=n_meta, n_taps=n_taps)
    row = lambda a: a.reshape(1, cw)
    return pl.pallas_call(
        kern,
        grid=(n_batch, seq // tt),
        in_specs=[pl.BlockSpec((1, tt, cw), lambda b, t: (b, t, c_blk)),
                  pl.BlockSpec((1, tt, cw), lambda b, t: (b, t, c_blk + 1)),
                  pl.BlockSpec((n_meta, cw), lambda b, t: (0, c_blk)),
                  pl.BlockSpec((n_meta, cw), lambda b, t: (0, c_blk + 1)),
                  pl.BlockSpec((n_taps, cw), lambda b, t: (0, 0)),
                  pl.BlockSpec((1, cw), lambda b, t: (0, 0)),
                  pl.BlockSpec((1, cw), lambda b, t: (0, 0)),
                  pl.BlockSpec((1, cw), lambda b, t: (0, 0))],
        out_specs=[pl.BlockSpec((1, tt, cw), lambda b, t: (b, t, 0)),
                   pl.BlockSpec((1, n_taps - 1, cw), lambda b, t: (b, 0, 0))],
        out_shape=[jax.ShapeDtypeStruct((n_batch, seq, cw), BF16),
                   jax.ShapeDtypeStruct((n_batch, n_taps - 1, cw), F32)],
        scratch_shapes=[pltpu.VMEM((HIST + tt, cw), F32), pltpu.VMEM((tt, cw), F32)],
        compiler_params=_cparams("parallel", "arbitrary"),
        name="prompt_conv",
    )(z3, z3, zm, zm, conv_w, row(conv_b), row(ln_g), row(ln_b))


def _sample_conv_kernel(c1_ref, c2_ref, st_ref, w_ref, cb_ref, g_ref, b_ref, y_ref, tail_ref, ext,
                        *, t_new, n_taps):
    hist = n_taps - 1
    ext[0:hist, :] = st_ref[0]
    ext[hist:hist + t_new, :] = c1_ref[0] * _sigmoid(c2_ref[0])
    acc = jnp.zeros((t_new, ext.shape[1]), F32)
    for j in range(n_taps):
        acc = acc + w_ref[j:j + 1, :] * ext[j:j + t_new, :]
    y = _layer_norm(acc + cb_ref[...], g_ref[...], b_ref[...])
    y_ref[0] = (y * _sigmoid(y)).astype(y_ref.dtype)
    tail_ref[0] = ext[t_new:t_new + hist, :]


def _sample_conv(z3, state, conv_w, conv_b, ln_g, ln_b, *, c_blk):
    n_batch, t_new, _ = z3.shape
    n_taps, cw = conv_w.shape
    kern = functools.partial(_sample_conv_kernel, t_new=t_new, n_taps=n_taps)
    row = lambda a: a.reshape(1, cw)
    return pl.pallas_call(
        kern,
        grid=(n_batch,),
        in_specs=[pl.BlockSpec((1, t_new, cw), lambda b: (b, 0, c_blk)),
                  pl.BlockSpec((1, t_new, cw), lambda b: (b, 0, c_blk + 1)),
                  pl.BlockSpec((1, n_taps - 1, cw), lambda b: (b, 0, 0)),
                  pl.BlockSpec((n_taps, cw), lambda b: (0, 0)),
                  pl.BlockSpec((1, cw), lambda b: (0, 0)),
                  pl.BlockSpec((1, cw), lambda b: (0, 0)),
                  pl.BlockSpec((1, cw), lambda b: (0, 0))],
        out_specs=[pl.BlockSpec((1, t_new, cw), lambda b: (b, 0, 0)),
                   pl.BlockSpec((1, n_taps - 1, cw), lambda b: (b, 0, 0))],
        out_shape=[jax.ShapeDtypeStruct((n_batch, t_new, cw), F32),
                   jax.ShapeDtypeStruct((n_batch, n_taps - 1, cw), F32)],
        scratch_shapes=[pltpu.VMEM((n_taps - 1 + t_new + 6, cw), F32)],
        compiler_params=_cparams("parallel"),
        name="sample_conv",
    )(z3, z3, state, conv_w, row(conv_b), row(ln_g), row(ln_b))


PAGES_PER_STEP = 4


class _PagedAttn:
    def __init__(self, refs, scratch, *, page, t_new, n_heads, out_scale):
        n_pg = PAGES_PER_STEP
        self.lam_ref, self.q_ref = refs[:2]
        self.k_refs = refs[2:2 + n_pg]
        self.v_refs = refs[2 + n_pg:2 + 2 * n_pg]
        self.kn_ref, self.vn_ref, self.g_ref, self.spread_ref, self.head_ref, self.o_ref = refs[2 + 2 * n_pg:]
        self.m_ref, self.l_ref, self.acc_ref = scratch
        self.page, self.t_new, self.n_heads, self.out_scale = page, t_new, n_heads, out_scale
        self.grp = 2 * t_new

    def scores(self, key_refs, mask=None):
        q = self.q_ref[0]
        s = jnp.concatenate([jnp.dot(q, kr[0].astype(BF16), preferred_element_type=F32)
                             for kr in key_refs], axis=1)
        return s if mask is None else jnp.where(mask, s, NEG)

    def weights(self, s, *, first=None, commit=None):
        restart = (lambda ref, v: ref[...]) if first is None else (lambda ref, v: jnp.where(first, v, ref[...]))
        keep = (lambda new, old: new) if commit is None else (lambda new, old: jnp.where(commit, new, old))
        m_old = restart(self.m_ref, NEG)
        m_new = jnp.maximum(m_old, jnp.max(s, axis=-1, keepdims=True))
        alpha = jnp.exp(m_old - m_new)
        pe = jnp.exp(s - m_new)
        l_new = alpha * restart(self.l_ref, 0.0) + jnp.sum(pe, axis=-1, keepdims=True)
        self.l_ref[...] = keep(l_new, self.l_ref[...])
        self.m_ref[...] = keep(m_new, self.m_ref[...])
        return alpha, pe

    def values(self, alpha, pe, val_refs, *, first=None, commit=None):
        page, rows = self.page, self.acc_ref.shape[0]
        pe_b = pe.astype(BF16)
        stacked = jnp.concatenate([pe_b[:, i * page:(i + 1) * page] for i in range(len(val_refs))], axis=0)
        spread = jnp.dot(stacked, self.spread_ref[...], preferred_element_type=F32)
        p2 = jnp.concatenate([(spread[i * rows:(i + 1) * rows] * self.head_ref[...]).astype(BF16)
                              for i in range(len(val_refs))], axis=1)
        v = jnp.concatenate([vr[0].astype(BF16) for vr in val_refs], axis=0)
        acc_old = self.acc_ref[...]
        if first is not None:
            acc_old = jnp.where(first, 0.0, acc_old)
        acc_new = alpha * acc_old + jnp.dot(p2, v, preferred_element_type=F32)
        self.acc_ref[...] = acc_new if commit is None else jnp.where(commit, acc_new, self.acc_ref[...])

    def finish(self):
        t_new, grp = self.t_new, self.grp
        rows = grp * self.n_heads
        r = lax.broadcasted_iota(jnp.int32, (rows, LANES), 0)
        c = lax.broadcasted_iota(jnp.int32, (rows, LANES), 1)
        alpha, pe = self.weights(self.scores([self.kn_ref], mask=c <= r % t_new))
        self.values(alpha, pe, [self.vn_ref])
        lam = self.lam_ref[0]
        for h in range(self.n_heads):
            r1 = h * grp
            r2 = r1 + t_new
            o1 = self.acc_ref[r1:r1 + t_new, :] / self.l_ref[r1:r1 + t_new, :]
            o2 = self.acc_ref[r2:r2 + t_new, :] / self.l_ref[r2:r2 + t_new, :]
            o = o1 - lam * o2
            ms = jnp.mean(o * o, axis=-1, keepdims=True)
            self.o_ref[0, :, h * LANES:(h + 1) * LANES] = (
                o * lax.rsqrt(ms + LN_EPS) * self.g_ref[...] * self.out_scale)


def _mix_router_kernel(a_ref, y_ref, g1_ref, g2_ref, x_ref, wa_ref, wc_ref, bc_ref, wo_ref,
                       lg_ref, lb_ref, wrh_ref, wrl_ref, eb_ref, h_ref, gate_ref, *, alpha, n_exp):
    att = jnp.dot(a_ref[...].astype(BF16), wa_ref[...], preferred_element_type=F32)
    cnv = jnp.dot(y_ref[...].astype(BF16), wc_ref[...], preferred_element_type=F32) + bc_ref[...]
    mix = _sigmoid(g1_ref[...]) * att + _sigmoid(g2_ref[...]) * cnv
    res = alpha * x_ref[...] + jnp.dot(mix.astype(BF16), wo_ref[...], preferred_element_type=F32)
    h = _layer_norm(res, lg_ref[...], lb_ref[...])
    h_ref[...] = h

    h_hi = h.astype(BF16)
    h_lo = (h - h_hi.astype(F32)).astype(BF16)
    logits = (lax.dot_general(wrh_ref[...], h_hi, _NT, preferred_element_type=F32)
              + lax.dot_general(wrh_ref[...], h_lo, _NT, preferred_element_type=F32)
              + lax.dot_general(wrl_ref[...], h_hi, _NT, preferred_element_type=F32))
    tm = logits.shape[1]
    s = _sigmoid(logits)
    sc = s + eb_ref[...]
    gsz = n_exp // N_GROUPS

    scg = sc.reshape(N_GROUPS, gsz, tm)
    within = lax.broadcasted_iota(jnp.int32, scg.shape, 1)
    m1 = jnp.max(scg, axis=1, keepdims=True)
    first = jnp.min(jnp.where(scg == m1, within, gsz), axis=1, keepdims=True)
    m2 = jnp.max(jnp.where(within == first, -jnp.inf, scg), axis=1, keepdims=True)
    gs = (m1 + m2).reshape(N_GROUPS, tm)

    def rank_rows(x):
        n = x.shape[0]
        idx = lax.broadcasted_iota(jnp.int32, x.shape, 0)
        cnt = jnp.zeros(x.shape, F32)
        for j in range(n):
            row = x[j:j + 1, :]
            cnt = cnt + jnp.where(row > x, 1.0, jnp.where(row == x, (idx > j).astype(F32), 0.0))
        return cnt

    gsel = rank_rows(gs) < TOPK_GROUPS
    emask = jnp.broadcast_to(gsel.astype(F32).reshape(N_GROUPS, 1, tm), scg.shape).reshape(n_exp, tm)
    scm = jnp.where(emask > 0.5, sc, NEG)
    sel = rank_rows(scm) < TOP_K
    w = jnp.where(sel, s, 0.0)
    gate_t = w / jnp.sum(w, axis=0, keepdims=True) * ROUTED_SCALE
    pad = jnp.zeros((LANES - n_exp, tm), F32)
    gate_ref[...] = jnp.concatenate([gate_t, pad], axis=0).T


def _mix_router(a, yact, z, x, w_attn_o, w_conv_o, b_conv_o, w_out, ln_g, ln_b, wr_hi, wr_lo, e_bias,
                *, g_blk, alpha, tm):
    t, d = x.shape
    n_exp = e_bias.shape[0]
    kern = functools.partial(_mix_router_kernel, alpha=alpha, n_exp=n_exp)
    row = lambda v: v.reshape(1, d)
    tile = lambda c: pl.BlockSpec((tm, d), lambda i: (i, c))
    full = lambda shape: pl.BlockSpec(shape, lambda i: (0,) * len(shape))
    return pl.pallas_call(
        kern,
        grid=(t // tm,),
        in_specs=[tile(0), tile(0), tile(g_blk), tile(g_blk + 1), tile(0),
                  full((d, d)), full((d, d)), full((1, d)), full((d, d)), full((1, d)), full((1, d)),
                  full((n_exp, d)), full((n_exp, d)), full((n_exp, 1))],
        out_specs=[tile(0), pl.BlockSpec((tm, LANES), lambda i: (i, 0))],
        out_shape=[jax.ShapeDtypeStruct((t, d), F32), jax.ShapeDtypeStruct((t, LANES), F32)],
        compiler_params=_cparams("parallel"),
        name="mix_router",
    )(a, yact, z, z, x, w_attn_o, w_conv_o, row(b_conv_o), w_out, row(ln_g), row(ln_b),
      wr_hi, wr_lo, e_bias.reshape(n_exp, 1))


N_MOE_IN = 10
N_ATTN_IN = 2 + 2 * PAGES_PER_STEP + 5


def _moe_kernel(*refs, alpha, n_exp, attn):
    if attn is None:
        moe_in, (o_ref, xb_ref, acc_ref) = refs[:N_MOE_IN], refs[N_MOE_IN:]
        pa = None
    else:
        refs = refs[1:]
        moe_in = refs[:N_MOE_IN]
        attn_in = refs[N_MOE_IN:N_MOE_IN + N_ATTN_IN]
        o_ref, a_ref, xb_ref, acc_ref, m_ref, l_ref, pacc_ref = refs[N_MOE_IN + N_ATTN_IN:]
        pa = _PagedAttn(attn_in + (a_ref,), (m_ref, l_ref, pacc_ref), page=attn["page"],
                        t_new=attn["t_new"], n_heads=attn["n_heads"], out_scale=attn["out_scale"])
    h_ref, gate_ref, wg_ref, wu_ref, wd_ref, sg_ref, su_ref, sd_ref, lg_ref, lb_ref = moe_in
    e = pl.program_id(1)

    @pl.when(e == 0)
    def _():
        xb_ref[...] = h_ref[...].astype(BF16)
        acc_ref[...] = jnp.zeros(acc_ref.shape, F32)

    def expert(wg, wu, wd, gcol):
        if pa is not None:
            step = pl.program_id(0) * pl.num_programs(1) + e
            state = dict(first=step % attn["groups_per_seq"] == 0, commit=step < attn["n_groups"])
            s = pa.scores(pa.k_refs)
        x = xb_ref[...]
        gt = jnp.dot(x, wg, preferred_element_type=F32)
        if pa is not None:
            weights = pa.weights(s, **state)
        up = jnp.dot(x, wu, preferred_element_type=F32)
        if pa is not None:
            pa.values(*weights, pa.v_refs, **state)
        hid = gt * _sigmoid(gt) * up
        if gcol is not None:
            hid = hid * gcol
        acc_ref[...] += jnp.dot(hid.astype(BF16), wd, preferred_element_type=F32)

    @pl.when(e < n_exp)
    def _():
        lane = lax.broadcasted_iota(jnp.int32, gate_ref.shape, 1)
        gcol = jnp.sum(jnp.where(lane == e, gate_ref[...], 0.0), axis=-1, keepdims=True)
        expert(wg_ref[0], wu_ref[0], wd_ref[0], gcol)

    @pl.when(e == n_exp)
    def _():
        expert(sg_ref[...], su_ref[...], sd_ref[...], None)
        o_ref[...] = _layer_norm(alpha * h_ref[...] + acc_ref[...], lg_ref[...], lb_ref[...])

    if pa is not None:
        step = pl.program_id(0) * pl.num_programs(1) + e
        last_of_seq = step % attn["groups_per_seq"] == attn["groups_per_seq"] - 1

        @pl.when(jnp.logical_and(step < attn["n_groups"], last_of_seq))
        def _():
            pa.finish()


def _moe(h, gate, w_gate, w_up, w_down, ws_gate, ws_up, ws_down, ln_g, ln_b, *, alpha, tm, paged=None):
    t, d = h.shape
    n_exp, _, f = w_gate.shape
    n_steps = n_exp + 1
    grid = (t // tm, n_steps)
    routed = lambda shape: pl.BlockSpec((1,) + shape, lambda i, e, *_: (jnp.minimum(e, n_exp - 1), 0, 0))
    const = lambda shape: pl.BlockSpec(shape, lambda i, e, *_: (0,) * len(shape))
    tile = lambda w: pl.BlockSpec((tm, w), lambda i, e, *_: (i, 0))
    in_specs = [tile(d), tile(LANES), routed((d, f)), routed((d, f)), routed((f, d)),
                const((d, f)), const((d, f)), const((f, d)), const((1, d)), const((1, d))]
    args = [h, gate, w_gate, w_up, w_down, ws_gate, ws_up, ws_down, ln_g.reshape(1, d), ln_b.reshape(1, d)]
    out_specs = [tile(d)]
    out_shape = [jax.ShapeDtypeStruct((t, d), F32)]
    scratch = [pltpu.VMEM((tm, d), BF16), pltpu.VMEM((tm, d), F32)]
    if paged is None:
        kern = functools.partial(_moe_kernel, alpha=alpha, n_exp=n_exp, attn=None)
        return pl.pallas_call(
            kern, grid=grid, in_specs=in_specs, out_specs=out_specs, out_shape=out_shape,
            scratch_shapes=scratch, compiler_params=_cparams("parallel", "arbitrary"), name="moe",
        )(*args)[0]

    page_table, lam, qbd, cache_kt, cache_v, knew_t, vnew, subln_g, t_new, lam_init = paged
    n_seq, n_pages = page_table.shape
    _, width, page = cache_kt.shape
    v_rows, v_dim = cache_v.shape[1:]
    n_heads = v_rows // page
    rows = qbd.shape[1]
    n_pg = PAGES_PER_STEP
    gps = n_pages // n_pg
    n_groups = n_seq * gps
    assert n_pages % n_pg == 0 and n_groups <= grid[0] * n_steps, "not enough MoE steps to host the pages"

    def seq_and_group(i, e):
        g = jnp.minimum(i * n_steps + e, n_groups - 1)
        return g // gps, g % gps

    def page_map(k):
        def index(i, e, pt):
            b, p = seq_and_group(i, e)
            return (pt[b, p * n_pg + k], 0, 0)
        return index

    per_seq = lambda shape: pl.BlockSpec((1,) + shape, lambda i, e, pt: (seq_and_group(i, e)[0], 0, 0))
    in_specs += ([pl.BlockSpec(memory_space=pltpu.SMEM), per_seq((rows, width))]
                 + [pl.BlockSpec((1, width, page), page_map(k)) for k in range(n_pg)]
                 + [pl.BlockSpec((1, v_rows, v_dim), page_map(k)) for k in range(n_pg)]
                 + [per_seq((width, page)), per_seq((v_rows, v_dim)), const((1, v_dim)),
                    const((page, v_rows)), const((rows, v_rows))])
    v_row = jnp.arange(v_rows)[None, :]
    spread = (v_row // n_heads == jnp.arange(page)[:, None]).astype(BF16)
    head_mask = (v_row % n_heads == jnp.arange(rows)[:, None] // (rows // n_heads)).astype(F32)
    args += [lam, qbd] + [cache_kt] * n_pg + [cache_v] * n_pg + [knew_t, vnew, subln_g, spread, head_mask]
    out_specs.append(per_seq((t_new, n_heads * v_dim)))
    out_shape.append(jax.ShapeDtypeStruct((n_seq, t_new, n_heads * v_dim), F32))
    scratch += [pltpu.VMEM((rows, 1), F32), pltpu.VMEM((rows, 1), F32), pltpu.VMEM((rows, v_dim), F32)]
    attn = dict(page=page, t_new=t_new, n_heads=n_heads, out_scale=1.0 - lam_init,
                n_groups=n_groups, groups_per_seq=gps)
    kern = functools.partial(_moe_kernel, alpha=alpha, n_exp=n_exp, attn=attn)
    return pl.pallas_call(
        kern,
        grid_spec=pltpu.PrefetchScalarGridSpec(num_scalar_prefetch=1, grid=grid, in_specs=in_specs,
                                               out_specs=out_specs, scratch_shapes=scratch),
        out_shape=out_shape,
        compiler_params=_cparams("arbitrary", "arbitrary"),
        name="moe_paged_attn",
    )(page_table, *args)


def kernel(x_prompt, x_sample, cache_k, cache_v, state_conv, page_table, meta_tokens, w_in, b_in, lq1, lk1, lq2, lk2, subln_g, w_attn_o, conv_w, conv_b, conv_ln_g, conv_ln_b, w_conv_o, b_conv_o, w_out, ln1_g, ln1_b, ln2_g, ln2_b, w_router, e_bias, w_gate, w_up, w_down, ws_gate, ws_up, ws_down):
    depth = w_in.shape[0]
    assert depth == 1, "single-layer step"
    n_batch, seq, d = x_prompt.shape
    n_dec, t_new, _ = x_sample.shape
    _, n_pool, page, n_heads, n_maps, half = cache_k.shape
    n_meta = meta_tokens.shape[0]
    qk_w = n_heads * n_maps * half
    v_w = cache_v.shape[-1] * n_heads
    cw = conv_w.shape[-1]
    in_w = w_in.shape[-1]
    assert n_maps == 2 and 2 * half == LANES and cache_v.shape[-1] == LANES and page == LANES
    assert qk_w == d and v_w == d and cw == d and in_w == 7 * d
    assert n_meta <= HIST and n_meta % 8 == 0 and conv_w.shape[1] - 1 <= HIST
    assert page_table.shape[1] % PAGES_PER_STEP == 0
    c_blk, g_blk = 3, 5
    alpha = (2.0 * depth) ** 0.25
    lam_init = 0.8 - 0.6 * math.exp(-0.3 * 0)
    f32 = lambda a: a.astype(F32)
    lam = (jnp.exp(jnp.sum(f32(lq1[0]) * f32(lk1[0]))) - jnp.exp(jnp.sum(f32(lq2[0]) * f32(lk2[0])))
           + lam_init).reshape(1)

    w_in_b = w_in[0].astype(BF16)
    b_in_r = b_in[0].reshape(1, in_w)
    w_attn_o_b = w_attn_o[0].astype(BF16)
    w_conv_o_b = w_conv_o[0].astype(BF16)
    w_out_b = w_out[0].astype(BF16)
    wr_t = w_router[0].T
    wr_hi = wr_t.astype(BF16)
    wr_lo = (wr_t - wr_hi.astype(F32)).astype(BF16)
    sub_g = subln_g[0].reshape(1, LANES)
    bf = lambda w: w[0].astype(BF16)
    moe = functools.partial(_moe, w_gate=bf(w_gate), w_up=bf(w_up), w_down=bf(w_down), ws_gate=bf(ws_gate),
                            ws_up=bf(ws_up), ws_down=bf(ws_down), ln_g=ln2_g[0], ln_b=ln2_b[0], alpha=alpha)

    xs = x_sample.reshape(n_dec * t_new, d)
    zs = _inproj(xs, w_in_b, b_in_r, n_dec * t_new, d)
    zs3 = zs.reshape(n_dec, t_new, in_w)
    n_hm = n_heads * n_maps
    q_rep = jnp.tile(zs3[:, :, :qk_w] * half ** -0.5, (1, n_hm, 1))
    own = (jnp.arange(n_hm * t_new)[:, None] // t_new) == (jnp.arange(qk_w)[None, :] // half)
    qbd = jnp.where(own[None], q_rep, 0.0).astype(BF16)
    k_new = zs3[:, :, qk_w:2 * qk_w]
    v_new = zs3[:, :, 2 * qk_w:2 * qk_w + v_w]
    knew_t = jnp.pad(k_new.transpose(0, 2, 1), ((0, 0), (0, 0), (0, page - t_new)))
    vnew_p = jnp.pad(v_new.reshape(n_dec, t_new * n_heads, LANES), ((0, 0), (0, (page - t_new) * n_heads), (0, 0)))
    cache_kt = jnp.transpose(cache_k[0], (0, 2, 3, 4, 1)).reshape(n_pool, qk_w, page)
    cache_vr = cache_v[0].reshape(n_pool, page * n_heads, LANES)
    paged = (page_table, lam, qbd, cache_kt, cache_vr, knew_t, vnew_p, sub_g, t_new, lam_init)

    xp = x_prompt.reshape(n_batch * seq, d)
    z = _inproj(xp, w_in_b, b_in_r, min(2048, n_batch * seq), 512)
    zm = _inproj(meta_tokens.astype(F32), w_in_b, b_in_r, n_meta, d)
    zm_pad = jnp.pad(zm, ((0, LANES - n_meta), (0, 0)))
    a_p, k_all, v_all = _prompt_attention(z, zm_pad, lam, sub_g, n_batch=n_batch, seq=seq, n_heads=n_heads,
                                          n_meta=n_meta, half=half, lam_init=lam_init, tq=512)
    z3 = z.reshape(n_batch, seq, in_w)
    yact_p, conv_prompt = _prompt_conv(z3, zm, conv_w[0], conv_b[0], conv_ln_g[0], conv_ln_b[0],
                                       c_blk=c_blk, tt=256)
    h_p, gate_p = _mix_router(a_p, yact_p.reshape(n_batch * seq, cw), z, xp, w_attn_o_b, w_conv_o_b,
                              b_conv_o[0], w_out_b, ln1_g[0], ln1_b[0], wr_hi, wr_lo, e_bias[0],
                              g_blk=g_blk, alpha=alpha, tm=512)
    y_p, a_s = moe(h_p, gate_p, tm=min(1024, n_batch * seq), paged=paged)
    k_prompt = k_all.reshape(1, n_batch, n_meta + seq, n_heads, n_maps, half)
    v_prompt = v_all.reshape(1, n_batch, n_meta + seq, n_heads, LANES)

    yact_s, conv_sample = _sample_conv(zs3, state_conv[0], conv_w[0], conv_b[0], conv_ln_g[0],
                                       conv_ln_b[0], c_blk=c_blk)
    h_s, gate_s = _mix_router(a_s.reshape(n_dec * t_new, v_w), yact_s.reshape(n_dec * t_new, cw), zs, xs,
                              w_attn_o_b, w_conv_o_b, b_conv_o[0], w_out_b, ln1_g[0], ln1_b[0],
                              wr_hi, wr_lo, e_bias[0], g_blk=g_blk, alpha=alpha, tm=n_dec * t_new)
    y_s = moe(h_s, gate_s, tm=n_dec * t_new)

    k_sample = k_new.reshape(1, n_dec, t_new, n_heads, n_maps, half)
    v_sample = v_new.reshape(1, n_dec, t_new, n_heads, LANES)
    return (y_p.reshape(n_batch, seq, d), y_s.reshape(n_dec, t_new, d), k_prompt, v_prompt,
            conv_prompt[None], k_sample, v_sample, conv_sample[None])
```

```python
import functools
import math

import jax
import jax.numpy as jnp
from jax import lax
from jax.experimental import pallas as pl
from jax.experimental.pallas import tpu as pltpu

F32 = jnp.float32
BF16 = jnp.bfloat16

LN_EPS = 1e-5
NEG = -1e30
N_GROUPS = 8
TOPK_GROUPS = 4
TOP_K = 8
ROUTED_SCALE = 2.5
LANES = 128
SUBLANES = 8
VMEM_LIMIT = 48 * 1024 * 1024

_NT = (((1,), (1,)), ((), ()))


def _cparams(*sem):
    return pltpu.CompilerParams(dimension_semantics=sem, vmem_limit_bytes=VMEM_LIMIT)


def _sigmoid(x):
    return 1.0 / (1.0 + jnp.exp(-x))


def _layer_norm(x, g, b):
    mu = jnp.mean(x, axis=-1, keepdims=True)
    xc = x - mu
    var = jnp.mean(xc * xc, axis=-1, keepdims=True)
    return xc * lax.rsqrt(var + LN_EPS) * g + b


def _inproj_kernel(x_ref, w_ref, b_ref, o_ref):
    x = x_ref[...].astype(BF16)
    o_ref[...] = jnp.dot(x, w_ref[...], preferred_element_type=F32) + b_ref[...]


def _inproj(x, w, b, tm, tn):
    m, k = x.shape
    n = w.shape[1]
    return pl.pallas_call(
        _inproj_kernel,
        grid=(m // tm, n // tn),
        in_specs=[pl.BlockSpec((tm, k), lambda i, j: (i, 0)),
                  pl.BlockSpec((k, tn), lambda i, j: (0, j)),
                  pl.BlockSpec((1, tn), lambda i, j: (0, j))],
        out_specs=pl.BlockSpec((tm, tn), lambda i, j: (i, j)),
        out_shape=jax.ShapeDtypeStruct((m, n), F32),
        compiler_params=_cparams("parallel", "parallel"),
        name="inproj",
    )(x, w, b)


def _fold_lanes(x, op):
    out = x[:, :LANES]
    for c in range(1, x.shape[1] // LANES):
        out = op(out, x[:, c * LANES:(c + 1) * LANES])
    return out


def _prompt_attn_kernel(lam_ref, q_ref, k_ref, v_ref, km_ref, vm_ref, g_ref, o_ref, ko_ref, vo_ref,
                        s_ref, sm_ref, red_ref, acc_ref, *, tq, n_meta, half, scale, out_scale):
    qi = pl.program_id(2)
    lam = lam_ref[0]

    @pl.when(qi == 0)
    def _():
        ko_ref[0, :n_meta, :] = km_ref[:n_meta, :]
        ko_ref[0, n_meta:, :] = k_ref[...]
        vo_ref[0, :n_meta, :] = vm_ref[:n_meta, :]
        vo_ref[0, n_meta:, :] = v_ref[...]

    q = q_ref[...] * (scale * math.log2(math.e))
    lane = lax.broadcasted_iota(jnp.int32, (1, LANES), 1)
    first_map = lane < half
    qs = (jnp.where(first_map, q, 0.0).astype(BF16), jnp.where(first_map, 0.0, q).astype(BF16))

    km = km_ref[...].astype(BF16)
    meta_cols = lax.broadcasted_iota(jnp.int32, (tq, LANES), 1) < n_meta
    for mp in range(2):
        s = lax.dot_general(qs[mp], km, _NT, preferred_element_type=F32)
        s = jnp.where(meta_cols, s, NEG)
        sm_ref[mp] = s
        red_ref[mp] = s

    def score_block(j, masked):
        kb = k_ref[pl.ds(pl.multiple_of(j * tq, tq), tq), :].astype(BF16)
        for mp in range(2):
            s = lax.dot_general(qs[mp], kb, _NT, preferred_element_type=F32)
            if masked:
                r = lax.broadcasted_iota(jnp.int32, (tq, tq), 0)
                c = lax.broadcasted_iota(jnp.int32, (tq, tq), 1)
                s = jnp.where(c <= r, s, NEG)
            s_ref[mp, j] = s
            red_ref[mp] = jnp.maximum(red_ref[mp], _fold_lanes(s, jnp.maximum))

    def full_block(j, carry):
        score_block(j, False)
        return carry

    lax.fori_loop(0, qi, full_block, 0)
    score_block(qi, True)
    mx = [jnp.max(red_ref[mp], axis=-1, keepdims=True) for mp in range(2)]

    def with_ones(v):
        return jnp.concatenate([v.astype(BF16), jnp.ones(v.shape, BF16)], axis=1)

    def weights(load):
        return jnp.concatenate([jnp.exp2(load(mp) - mx[mp]).astype(BF16) for mp in range(2)], axis=0)

    acc_ref[...] = jnp.dot(weights(lambda mp: sm_ref[mp]), with_ones(vm_ref[...]),
                           preferred_element_type=F32)

    def pv_block(j, carry):
        vb = v_ref[pl.ds(pl.multiple_of(j * tq, tq), tq), :]
        acc_ref[...] += jnp.dot(weights(lambda mp: s_ref[mp, j]), with_ones(vb),
                                preferred_element_type=F32)
        return carry

    lax.fori_loop(0, qi + 1, pv_block, 0)
    o = (acc_ref[:tq, :LANES] / acc_ref[:tq, LANES:]
         - lam * (acc_ref[tq:, :LANES] / acc_ref[tq:, LANES:]))
    ms = jnp.mean(o * o, axis=-1, keepdims=True)
    o_ref[...] = (o * lax.rsqrt(ms + LN_EPS) * g_ref[...] * out_scale).astype(o_ref.dtype)


def _prompt_attention(z, zm_pad, lam, subln_g, *, n_batch, seq, n_heads, n_meta, half, lam_init, tq):
    nq = seq // tq
    kern = functools.partial(_prompt_attn_kernel, tq=tq, n_meta=n_meta, half=half,
                             scale=half ** -0.5, out_scale=1.0 - lam_init)
    kv_out = pl.BlockSpec((1, n_meta + seq, LANES), lambda b, h, i: (b, 0, h))
    kv_shape = jax.ShapeDtypeStruct((n_batch, n_meta + seq, n_heads * LANES), F32)
    return pl.pallas_call(
        kern,
        grid=(n_batch, n_heads, nq),
        in_specs=[pl.BlockSpec(memory_space=pltpu.SMEM),
                  pl.BlockSpec((tq, LANES), lambda b, h, i: (b * nq + i, h)),
                  pl.BlockSpec((seq, LANES), lambda b, h, i: (b, n_heads + h)),
                  pl.BlockSpec((seq, LANES), lambda b, h, i: (b, 2 * n_heads + h)),
                  pl.BlockSpec((LANES, LANES), lambda b, h, i: (0, n_heads + h)),
                  pl.BlockSpec((LANES, LANES), lambda b, h, i: (0, 2 * n_heads + h)),
                  pl.BlockSpec((1, LANES), lambda b, h, i: (0, 0))],
        out_specs=[pl.BlockSpec((tq, LANES), lambda b, h, i: (b * nq + i, h)), kv_out, kv_out],
        out_shape=[jax.ShapeDtypeStruct((n_batch * seq, n_heads * LANES), BF16), kv_shape, kv_shape],
        scratch_shapes=[pltpu.VMEM((2, nq, tq, tq), F32),
                        pltpu.VMEM((2, tq, LANES), F32),
                        pltpu.VMEM((2, tq, LANES), F32),
                        pltpu.VMEM((2 * tq, 2 * LANES), F32)],
        compiler_params=_cparams("parallel", "parallel", "arbitrary"),
        name="prompt_attn",
    )(lam, z, z, z, zm_pad, zm_pad, subln_g)


HIST = 32
CONV_ROWS = 128


def _prompt_conv_kernel(c1_ref, c2_ref, m1_ref, m2_ref, w_ref, cb_ref, g_ref, b_ref,
                        y_ref, tail_ref, ubuf, ybuf, *, tt, n_meta, n_taps):
    t = pl.program_id(1)
    off = HIST - (n_taps - 1)

    @pl.when(t == 0)
    def _():
        ubuf[0:HIST - n_meta, :] = jnp.zeros((HIST - n_meta, ubuf.shape[1]), F32)
        ubuf[HIST - n_meta:HIST, :] = m1_ref[...] * _sigmoid(m2_ref[...])

    @pl.when(t > 0)
    def _():
        ubuf[0:HIST, :] = ubuf[tt:tt + HIST, :]

    ubuf[HIST:HIST + tt, :] = c1_ref[0] * _sigmoid(c2_ref[0])

    def lane_chunk(c, carry):
        cols = pl.ds(pl.multiple_of(c * LANES, LANES), LANES)
        for r0 in range(0, tt, CONV_ROWS):
            acc = None
            for shift in range(SUBLANES):
                part = None
                for j in range(n_taps):
                    if (off + j) % SUBLANES != shift:
                        continue
                    a = off + r0 + j - shift
                    rows = CONV_ROWS + (SUBLANES if shift else 0)
                    term = w_ref[j:j + 1, cols] * ubuf[a:a + rows, cols]
                    part = term if part is None else part + term
                if part is None:
                    continue
                part = part[shift:shift + CONV_ROWS]
                acc = part if acc is None else acc + part
            ybuf[r0:r0 + CONV_ROWS, cols] = acc
        return carry

    lax.fori_loop(0, ubuf.shape[1] // LANES, lane_chunk, 0)
    y = _layer_norm(ybuf[...] + cb_ref[...], g_ref[...], b_ref[...])
    y_ref[0] = (y * _sigmoid(y)).astype(y_ref.dtype)

    @pl.when(t == pl.num_programs(1) - 1)
    def _():
        tail_ref[0] = ubuf[HIST + tt - (n_taps - 1):HIST + tt, :]


def _prompt_conv(z3, zm, conv_w, conv_b, ln_g, ln_b, *, c_blk, tt):
    n_batch, seq, _ = z3.shape
    n_taps, cw = conv_w.shape
    n_meta = zm.shape[0]
    kern = functools.partial(_prompt_conv_kernel, tt=tt, n_meta=n_meta, n_taps=n_taps)
    row = lambda a: a.reshape(1, cw)
    return pl.pallas_call(
        kern,
        grid=(n_batch, seq // tt),
        in_specs=[pl.BlockSpec((1, tt, cw), lambda b, t: (b, t, c_blk)),
                  pl.BlockSpec((1, tt, cw), lambda b, t: (b, t, c_blk + 1)),
                  pl.BlockSpec((n_meta, cw), lambda b, t: (0, c_blk)),
                  pl.BlockSpec((n_meta, cw), lambda b, t: (0, c_blk + 1)),
                  pl.BlockSpec((n_taps, cw), lambda b, t: (0, 0)),
                  pl.BlockSpec((1, cw), lambda b, t: (0, 0)),
                  pl.BlockSpec((1, cw), lambda b, t: (0, 0)),
                  pl.BlockSpec((1, cw), lambda b, t: (0, 0))],
        out_specs=[pl.BlockSpec((1, tt, cw), lambda b, t: (b, t, 0)),
                   pl.BlockSpec((1, n_taps - 1, cw), lambda b, t: (b, 0, 0))],
        out_shape=[jax.ShapeDtypeStruct((n_batch, seq, cw), BF16),
                   jax.ShapeDtypeStruct((n_batch, n_taps - 1, cw), F32)],
        scratch_shapes=[pltpu.VMEM((HIST + tt, cw), F32), pltpu.VMEM((tt, cw), F32)],
        compiler_params=_cparams("parallel", "arbitrary"),
        name="prompt_conv",
    )(z3, z3, zm, zm, conv_w, row(conv_b), row(ln_g), row(ln_b))


def _sample_conv_kernel(c1_ref, c2_ref, st_ref, w_ref, cb_ref, g_ref, b_ref, y_ref, tail_ref, ext,
                        *, t_new, n_taps):
    hist = n_taps - 1
    ext[0:hist, :] = st_ref[0]
    ext[hist:hist + t_new, :] = c1_ref[0] * _sigmoid(c2_ref[0])
    acc = jnp.zeros((t_new, ext.shape[1]), F32)
    for j in range(n_taps):
        acc = acc + w_ref[j:j + 1, :] * ext[j:j + t_new, :]
    y = _layer_norm(acc + cb_ref[...], g_ref[...], b_ref[...])
    y_ref[0] = (y * _sigmoid(y)).astype(y_ref.dtype)
    tail_ref[0] = ext[t_new:t_new + hist, :]


def _sample_conv(z3, state, conv_w, conv_b, ln_g, ln_b, *, c_blk):
    n_batch, t_new, _ = z3.shape
    n_taps, cw = conv_w.shape
    kern = functools.partial(_sample_conv_kernel, t_new=t_new, n_taps=n_taps)
    row = lambda a: a.reshape(1, cw)
    return pl.pallas_call(
        kern,
        grid=(n_batch,),
        in_specs=[pl.BlockSpec((1, t_new, cw), lambda b: (b, 0, c_blk)),
                  pl.BlockSpec((1, t_new, cw), lambda b: (b, 0, c_blk + 1)),
                  pl.BlockSpec((1, n_taps - 1, cw), lambda b: (b, 0, 0)),
                  pl.BlockSpec((n_taps, cw), lambda b: (0, 0)),
                  pl.BlockSpec((1, cw), lambda b: (0, 0)),
                  pl.BlockSpec((1, cw), lambda b: (0, 0)),
                  pl.BlockSpec((1, cw), lambda b: (0, 0))],
        out_specs=[pl.BlockSpec((1, t_new, cw), lambda b: (b, 0, 0)),
                   pl.BlockSpec((1, n_taps - 1, cw), lambda b: (b, 0, 0))],
        out_shape=[jax.ShapeDtypeStruct((n_batch, t_new, cw), F32),
                   jax.ShapeDtypeStruct((n_batch, n_taps - 1, cw), F32)],
        scratch_shapes=[pltpu.VMEM((n_taps - 1 + t_new + 6, cw), F32)],
        compiler_params=_cparams("parallel"),
        name="sample_conv",
    )(z3, z3, state, conv_w, row(conv_b), row(ln_g), row(ln_b))


PAGES_PER_STEP = 4


class _PagedAttn:
    def __init__(self, refs, scratch, *, page, t_new, n_heads, out_scale):
        n_pg = PAGES_PER_STEP
        self.lam_ref, self.q_ref = refs[:2]
        self.k_refs = refs[2:2 + n_pg]
        self.v_refs = refs[2 + n_pg:2 + 2 * n_pg]
        self.kn_ref, self.vn_ref, self.g_ref, self.spread_ref, self.head_ref, self.o_ref = refs[2 + 2 * n_pg:]
        self.m_ref, self.l_ref, self.acc_ref = scratch
        self.page, self.t_new, self.n_heads, self.out_scale = page, t_new, n_heads, out_scale
        self.grp = 2 * t_new

    def scores(self, key_refs, mask=None):
        q = self.q_ref[0]
        s = jnp.concatenate([jnp.dot(q, kr[0].astype(BF16), preferred_element_type=F32)
                             for kr in key_refs], axis=1)
        return s if mask is None else jnp.where(mask, s, NEG)

    def weights(self, s, *, first=None, commit=None):
        restart = (lambda ref, v: ref[...]) if first is None else (lambda ref, v: jnp.where(first, v, ref[...]))
        keep = (lambda new, old: new) if commit is None else (lambda new, old: jnp.where(commit, new, old))
        m_old = restart(self.m_ref, NEG)
        m_new = jnp.maximum(m_old, jnp.max(s, axis=-1, keepdims=True))
        alpha = jnp.exp(m_old - m_new)
        pe = jnp.exp(s - m_new)
        l_new = alpha * restart(self.l_ref, 0.0) + jnp.sum(pe, axis=-1, keepdims=True)
        self.l_ref[...] = keep(l_new, self.l_ref[...])
        self.m_ref[...] = keep(m_new, self.m_ref[...])
        return alpha, pe

    def values(self, alpha, pe, val_refs, *, first=None, commit=None):
        page, rows = self.page, self.acc_ref.shape[0]
        pe_b = pe.astype(BF16)
        stacked = jnp.concatenate([pe_b[:, i * page:(i + 1) * page] for i in range(len(val_refs))], axis=0)
        spread = jnp.dot(stacked, self.spread_ref[...], preferred_element_type=F32)
        p2 = jnp.concatenate([(spread[i * rows:(i + 1) * rows] * self.head_ref[...]).astype(BF16)
                              for i in range(len(val_refs))], axis=1)
        v = jnp.concatenate([vr[0].astype(BF16) for vr in val_refs], axis=0)
        acc_old = self.acc_ref[...]
        if first is not None:
            acc_old = jnp.where(first, 0.0, acc_old)
        acc_new = alpha * acc_old + jnp.dot(p2, v, preferred_element_type=F32)
        self.acc_ref[...] = acc_new if commit is None else jnp.where(commit, acc_new, self.acc_ref[...])

    def finish(self):
        t_new, grp = self.t_new, self.grp
        rows = grp * self.n_heads
        r = lax.broadcasted_iota(jnp.int32, (rows, LANES), 0)
        c = lax.broadcasted_iota(jnp.int32, (rows, LANES), 1)
        alpha, pe = self.weights(self.scores([self.kn_ref], mask=c <= r % t_new))
        self.values(alpha, pe, [self.vn_ref])
        lam = self.lam_ref[0]
        for h in range(self.n_heads):
            r1 = h * grp
            r2 = r1 + t_new
            o1 = self.acc_ref[r1:r1 + t_new, :] / self.l_ref[r1:r1 + t_new, :]
            o2 = self.acc_ref[r2:r2 + t_new, :] / self.l_ref[r2:r2 + t_new, :]
            o = o1 - lam * o2
            ms = jnp.mean(o * o, axis=-1, keepdims=True)
            self.o_ref[0, :, h * LANES:(h + 1) * LANES] = (
                o * lax.rsqrt(ms + LN_EPS) * self.g_ref[...] * self.out_scale)


def _mix_router_kernel(a_ref, y_ref, g1_ref, g2_ref, x_ref, wa_ref, wc_ref, bc_ref, wo_ref,
                       lg_ref, lb_ref, wrh_ref, wrl_ref, eb_ref, h_ref, gate_ref, *, alpha, n_exp):
    att = jnp.dot(a_ref[...].astype(BF16), wa_ref[...], preferred_element_type=F32)
    cnv = jnp.dot(y_ref[...].astype(BF16), wc_ref[...], preferred_element_type=F32) + bc_ref[...]
    mix = _sigmoid(g1_ref[...]) * att + _sigmoid(g2_ref[...]) * cnv
    res = alpha * x_ref[...] + jnp.dot(mix.astype(BF16), wo_ref[...], preferred_element_type=F32)
    h = _layer_norm(res, lg_ref[...], lb_ref[...])
    h_ref[...] = h

    h_hi = h.astype(BF16)
    h_lo = (h - h_hi.astype(F32)).astype(BF16)
    logits = (lax.dot_general(wrh_ref[...], h_hi, _NT, preferred_element_type=F32)
              + lax.dot_general(wrh_ref[...], h_lo, _NT, preferred_element_type=F32)
              + lax.dot_general(wrl_ref[...], h_hi, _NT, preferred_element_type=F32))
    tm = logits.shape[1]
    s = _sigmoid(logits)
    sc = s + eb_ref[...]
    gsz = n_exp // N_GROUPS

    scg = sc.reshape(N_GROUPS, gsz, tm)
    within = lax.broadcasted_iota(jnp.int32, scg.shape, 1)
    m1 = jnp.max(scg, axis=1, keepdims=True)
    first = jnp.min(jnp.where(scg == m1, within, gsz), axis=1, keepdims=True)
    m2 = jnp.max(jnp.where(within == first, -jnp.inf, scg), axis=1, keepdims=True)
    gs = (m1 + m2).reshape(N_GROUPS, tm)

    def rank_rows(x):
        n = x.shape[0]
        idx = lax.broadcasted_iota(jnp.int32, x.shape, 0)
        cnt = jnp.zeros(x.shape, F32)
        for j in range(n):
            row = x[j:j + 1, :]
            cnt = cnt + jnp.where(row > x, 1.0, jnp.where(row == x, (idx > j).astype(F32), 0.0))
        return cnt

    gsel = rank_rows(gs) < TOPK_GROUPS
    emask = jnp.broadcast_to(gsel.astype(F32).reshape(N_GROUPS, 1, tm), scg.shape).reshape(n_exp, tm)
    scm = jnp.where(emask > 0.5, sc, NEG)
    sel = rank_rows(scm) < TOP_K
    w = jnp.where(sel, s, 0.0)
    gate_t = w / jnp.sum(w, axis=0, keepdims=True) * ROUTED_SCALE
    pad = jnp.zeros((LANES - n_exp, tm), F32)
    gate_ref[...] = jnp.concatenate([gate_t, pad], axis=0).T


def _mix_router(a, yact, z, x, w_attn_o, w_conv_o, b_conv_o, w_out, ln_g, ln_b, wr_hi, wr_lo, e_bias,
                *, g_blk, alpha, tm):
    t, d = x.shape
    n_exp = e_bias.shape[0]
    kern = functools.partial(_mix_router_kernel, alpha=alpha, n_exp=n_exp)
    row = lambda v: v.reshape(1, d)
    tile = lambda c: pl.BlockSpec((tm, d), lambda i: (i, c))
    full = lambda shape: pl.BlockSpec(shape, lambda i: (0,) * len(shape))
    return pl.pallas_call(
        kern,
        grid=(t // tm,),
        in_specs=[tile(0), tile(0), tile(g_blk), tile(g_blk + 1), tile(0),
                  full((d, d)), full((d, d)), full((1, d)), full((d, d)), full((1, d)), full((1, d)),
                  full((n_exp, d)), full((n_exp, d)), full((n_exp, 1))],
        out_specs=[tile(0), pl.BlockSpec((tm, LANES), lambda i: (i, 0))],
        out_shape=[jax.ShapeDtypeStruct((t, d), F32), jax.ShapeDtypeStruct((t, LANES), F32)],
        compiler_params=_cparams("parallel"),
        name="mix_router",
    )(a, yact, z, z, x, w_attn_o, w_conv_o, row(b_conv_o), w_out, row(ln_g), row(ln_b),
      wr_hi, wr_lo, e_bias.reshape(n_exp, 1))


N_MOE_IN = 10
N_ATTN_IN = 2 + 2 * PAGES_PER_STEP + 5


def _moe_kernel(*refs, alpha, n_exp, attn):
    if attn is None:
        moe_in, (o_ref, xb_ref, acc_ref) = refs[:N_MOE_IN], refs[N_MOE_IN:]
        pa = None
    else:
        refs = refs[1:]
        moe_in = refs[:N_MOE_IN]
        attn_in = refs[N_MOE_IN:N_MOE_IN + N_ATTN_IN]
        o_ref, a_ref, xb_ref, acc_ref, m_ref, l_ref, pacc_ref = refs[N_MOE_IN + N_ATTN_IN:]
        pa = _PagedAttn(attn_in + (a_ref,), (m_ref, l_ref, pacc_ref), page=attn["page"],
                        t_new=attn["t_new"], n_heads=attn["n_heads"], out_scale=attn["out_scale"])
    h_ref, gate_ref, wg_ref, wu_ref, wd_ref, sg_ref, su_ref, sd_ref, lg_ref, lb_ref = moe_in
    e = pl.program_id(1)

    @pl.when(e == 0)
    def _():
        xb_ref[...] = h_ref[...].astype(BF16)
        acc_ref[...] = jnp.zeros(acc_ref.shape, F32)

    def expert(wg, wu, wd, gcol):
        if pa is not None:
            step = pl.program_id(0) * pl.num_programs(1) + e
            state = dict(first=step % attn["groups_per_seq"] == 0, commit=step < attn["n_groups"])
            s = pa.scores(pa.k_refs)
        x = xb_ref[...]
        gt = jnp.dot(x, wg, preferred_element_type=F32)
        if pa is not None:
            weights = pa.weights(s, **state)
        up = jnp.dot(x, wu, preferred_element_type=F32)
        if pa is not None:
            pa.values(*weights, pa.v_refs, **state)
        hid = gt * _sigmoid(gt) * up
        if gcol is not None:
            hid = hid * gcol
        acc_ref[...] += jnp.dot(hid.astype(BF16), wd, preferred_element_type=F32)

    @pl.when(e < n_exp)
    def _():
        lane = lax.broadcasted_iota(jnp.int32, gate_ref.shape, 1)
        gcol = jnp.sum(jnp.where(lane == e, gate_ref[...], 0.0), axis=-1, keepdims=True)
        expert(wg_ref[0], wu_ref[0], wd_ref[0], gcol)

    @pl.when(e == n_exp)
    def _():
        expert(sg_ref[...], su_ref[...], sd_ref[...], None)
        o_ref[...] = _layer_norm(alpha * h_ref[...] + acc_ref[...], lg_ref[...], lb_ref[...])

    if pa is not None:
        step = pl.program_id(0) * pl.num_programs(1) + e
        last_of_seq = step % attn["groups_per_seq"] == attn["groups_per_seq"] - 1

        @pl.when(jnp.logical_and(step < attn["n_groups"], last_of_seq))
        def _():
            pa.finish()


def _moe(h, gate, w_gate, w_up, w_down, ws_gate, ws_up, ws_down, ln_g, ln_b, *, alpha, tm, paged=None):
    t, d = h.shape
    n_exp, _, f = w_gate.shape
    n_steps = n_exp + 1
    grid = (t // tm, n_steps)
    routed = lambda shape: pl.BlockSpec((1,) + shape, lambda i, e, *_: (jnp.minimum(e, n_exp - 1), 0, 0))
    const = lambda shape: pl.BlockSpec(shape, lambda i, e, *_: (0,) * len(shape))
    tile = lambda w: pl.BlockSpec((tm, w), lambda i, e, *_: (i, 0))
    in_specs = [tile(d), tile(LANES), routed((d, f)), routed((d, f)), routed((f, d)),
                const((d, f)), const((d, f)), const((f, d)), const((1, d)), const((1, d))]
    args = [h, gate, w_gate, w_up, w_down, ws_gate, ws_up, ws_down, ln_g.reshape(1, d), ln_b.reshape(1, d)]
    out_specs = [tile(d)]
    out_shape = [jax.ShapeDtypeStruct((t, d), F32)]
    scratch = [pltpu.VMEM((tm, d), BF16), pltpu.VMEM((tm, d), F32)]
    if paged is None:
        kern = functools.partial(_moe_kernel, alpha=alpha, n_exp=n_exp, attn=None)
        return pl.pallas_call(
            kern, grid=grid, in_specs=in_specs, out_specs=out_specs, out_shape=out_shape,
            scratch_shapes=scratch, compiler_params=_cparams("parallel", "arbitrary"), name="moe",
        )(*args)[0]

    page_table, lam, qbd, cache_kt, cache_v, knew_t, vnew, subln_g, t_new, lam_init = paged
    n_seq, n_pages = page_table.shape
    _, width, page = cache_kt.shape
    v_rows, v_dim = cache_v.shape[1:]
    n_heads = v_rows // page
    rows = qbd.shape[1]
    n_pg = PAGES_PER_STEP
    gps = n_pages // n_pg
    n_groups = n_seq * gps
    assert n_pages % n_pg == 0 and n_groups <= grid[0] * n_steps, "not enough MoE steps to host the pages"

    def seq_and_group(i, e):
        g = jnp.minimum(i * n_steps + e, n_groups - 1)
        return g // gps, g % gps

    def page_map(k):
        def index(i, e, pt):
            b, p = seq_and_group(i, e)
            return (pt[b, p * n_pg + k], 0, 0)
        return index

    per_seq = lambda shape: pl.BlockSpec((1,) + shape, lambda i, e, pt: (seq_and_group(i, e)[0], 0, 0))
    in_specs += ([pl.BlockSpec(memory_space=pltpu.SMEM), per_seq((rows, width))]
                 + [pl.BlockSpec((1, width, page), page_map(k)) for k in range(n_pg)]
                 + [pl.BlockSpec((1, v_rows, v_dim), page_map(k)) for k in range(n_pg)]
                 + [per_seq((width, page)), per_seq((v_rows, v_dim)), const((1, v_dim)),
                    const((page, v_rows)), const((rows, v_rows))])
    v_row = jnp.arange(v_rows)[None, :]
    spread = (v_row // n_heads == jnp.arange(page)[:, None]).astype(BF16)
    head_mask = (v_row % n_heads == jnp.arange(rows)[:, None] // (rows // n_heads)).astype(F32)
    args += [lam, qbd] + [cache_kt] * n_pg + [cache_v] * n_pg + [knew_t, vnew, subln_g, spread, head_mask]
    out_specs.append(per_seq((t_new, n_heads * v_dim)))
    out_shape.append(jax.ShapeDtypeStruct((n_seq, t_new, n_heads * v_dim), F32))
    scratch += [pltpu.VMEM((rows, 1), F32), pltpu.VMEM((rows, 1), F32), pltpu.VMEM((rows, v_dim), F32)]
    attn = dict(page=page, t_new=t_new, n_heads=n_heads, out_scale=1.0 - lam_init,
                n_groups=n_groups, groups_per_seq=gps)
    kern = functools.partial(_moe_kernel, alpha=alpha, n_exp=n_exp, attn=attn)
    return pl.pallas_call(
        kern,
        grid_spec=pltpu.PrefetchScalarGridSpec(num_scalar_prefetch=1, grid=grid, in_specs=in_specs,
                                               out_specs=out_specs, scratch_shapes=scratch),
        out_shape=out_shape,
        compiler_params=_cparams("arbitrary", "arbitrary"),
        name="moe_paged_attn",
    )(page_table, *args)


def kernel(x_prompt, x_sample, cache_k, cache_v, state_conv, page_table, meta_tokens, w_in, b_in, lq1, lk1, lq2, lk2, subln_g, w_attn_o, conv_w, conv_b, conv_ln_g, conv_ln_b, w_conv_o, b_conv_o, w_out, ln1_g, ln1_b, ln2_g, ln2_b, w_router, e_bias, w_gate, w_up, w_down, ws_gate, ws_up, ws_down):
    depth = w_in.shape[0]
    assert depth == 1, "single-layer step"
    n_batch, seq, d = x_prompt.shape
    n_dec, t_new, _ = x_sample.shape
    _, n_pool, page, n_heads, n_maps, half = cache_k.shape
    n_meta = meta_tokens.shape[0]
    qk_w = n_heads * n_maps * half
    v_w = cache_v.shape[-1] * n_heads
    cw = conv_w.shape[-1]
    in_w = w_in.shape[-1]
    assert n_maps == 2 and 2 * half == LANES and cache_v.shape[-1] == LANES and page == LANES
    assert qk_w == d and v_w == d and cw == d and in_w == 7 * d
    assert n_meta <= HIST and n_meta % 8 == 0 and conv_w.shape[1] - 1 <= HIST
    assert page_table.shape[1] % PAGES_PER_STEP == 0
    c_blk, g_blk = 3, 5
    alpha = (2.0 * depth) ** 0.25
    lam_init = 0.8 - 0.6 * math.exp(-0.3 * 0)
    f32 = lambda a: a.astype(F32)
    lam = (jnp.exp(jnp.sum(f32(lq1[0]) * f32(lk1[0]))) - jnp.exp(jnp.sum(f32(lq2[0]) * f32(lk2[0])))
           + lam_init).reshape(1)

    w_in_b = w_in[0].astype(BF16)
    b_in_r = b_in[0].reshape(1, in_w)
    w_attn_o_b = w_attn_o[0].astype(BF16)
    w_conv_o_b = w_conv_o[0].astype(BF16)
    w_out_b = w_out[0].astype(BF16)
    wr_t = w_router[0].T
    wr_hi = wr_t.astype(BF16)
    wr_lo = (wr_t - wr_hi.astype(F32)).astype(BF16)
    sub_g = subln_g[0].reshape(1, LANES)
    bf = lambda w: w[0].astype(BF16)
    moe = functools.partial(_moe, w_gate=bf(w_gate), w_up=bf(w_up), w_down=bf(w_down), ws_gate=bf(ws_gate),
                            ws_up=bf(ws_up), ws_down=bf(ws_down), ln_g=ln2_g[0], ln_b=ln2_b[0], alpha=alpha)

    xs = x_sample.reshape(n_dec * t_new, d)
    zs = _inproj(xs, w_in_b, b_in_r, n_dec * t_new, d)
    zs3 = zs.reshape(n_dec, t_new, in_w)
    n_hm = n_heads * n_maps
    q_rep = jnp.tile(zs3[:, :, :qk_w] * half ** -0.5, (1, n_hm, 1))
    own = (jnp.arange(n_hm * t_new)[:, None] // t_new) == (jnp.arange(qk_w)[None, :] // half)
    qbd = jnp.where(own[None], q_rep, 0.0).astype(BF16)
    k_new = zs3[:, :, qk_w:2 * qk_w]
    v_new = zs3[:, :, 2 * qk_w:2 * qk_w + v_w]
    knew_t = jnp.pad(k_new.transpose(0, 2, 1), ((0, 0), (0, 0), (0, page - t_new)))
    vnew_p = jnp.pad(v_new.reshape(n_dec, t_new * n_heads, LANES), ((0, 0), (0, (page - t_new) * n_heads), (0, 0)))
    cache_kt = jnp.transpose(cache_k[0], (0, 2, 3, 4, 1)).reshape(n_pool, qk_w, page)
    cache_vr = cache_v[0].reshape(n_pool, page * n_heads, LANES)
    paged = (page_table, lam, qbd, cache_kt, cache_vr, knew_t, vnew_p, sub_g, t_new, lam_init)

    xp = x_prompt.reshape(n_batch * seq, d)
    z = _inproj(xp, w_in_b, b_in_r, min(2048, n_batch * seq), 512)
    zm = _inproj(meta_tokens.astype(F32), w_in_b, b_in_r, n_meta, d)
    zm_pad = jnp.pad(zm, ((0, LANES - n_meta), (0, 0)))
    a_p, k_all, v_all = _prompt_attention(z, zm_pad, lam, sub_g, n_batch=n_batch, seq=seq, n_heads=n_heads,
                                          n_meta=n_meta, half=half, lam_init=lam_init, tq=1024)
    z3 = z.reshape(n_batch, seq, in_w)
    yact_p, conv_prompt = _prompt_conv(z3, zm, conv_w[0], conv_b[0], conv_ln_g[0], conv_ln_b[0],
                                       c_blk=c_blk, tt=256)
    h_p, gate_p = _mix_router(a_p, yact_p.reshape(n_batch * seq, cw), z, xp, w_attn_o_b, w_conv_o_b,
                              b_conv_o[0], w_out_b, ln1_g[0], ln1_b[0], wr_hi, wr_lo, e_bias[0],
                              g_blk=g_blk, alpha=alpha, tm=512)
    y_p, a_s = moe(h_p, gate_p, tm=min(1024, n_batch * seq), paged=paged)
    k_prompt = k_all.reshape(1, n_batch, n_meta + seq, n_heads, n_maps, half)
    v_prompt = v_all.reshape(1, n_batch, n_meta + seq, n_heads, LANES)

    yact_s, conv_sample = _sample_conv(zs3, state_conv[0], conv_w[0], conv_b[0], conv_ln_g[0],
                                       conv_ln_b[0], c_blk=c_blk)
    h_s, gate_s = _mix_router(a_s.reshape(n_dec * t_new, v_w), yact_s.reshape(n_dec * t_new, cw), zs, xs,
                              w_attn_o_b, w_conv_o_b, b_conv_o[0], w_out_b, ln1_g[0], ln1_b[0],
                              wr_hi, wr_lo, e_bias[0], g_blk=g_blk, alpha=alpha, tm=n_dec * t_new)
    y_s = moe(h_s, gate_s, tm=n_dec * t_new)

    k_sample = k_new.reshape(1, n_dec, t_new, n_heads, n_maps, half)
    v_sample = v_new.reshape(1, n_dec, t_new, n_heads, LANES)
    return (y_p.reshape(n_batch, seq, d), y_s.reshape(n_dec, t_new, d), k_prompt, v_prompt,
            conv_prompt[None], k_sample, v_sample, conv_sample[None])
```
